```python
import jax, jax.numpy as jnp
from jax import lax
import numpy as np

D_MODEL = 1024
BATCH = 16
SEQ = 2048
DEPTH = 2

N_MIXERS = 2
FOURIER_GROUPS = 4
RWKV_HEAD_SIZE = 64
RWKV_HEADS = D_MODEL // RWKV_HEAD_SIZE
DECAY_LORA = 64
AAA_LORA = 64
GATE_LORA = 128
N_DIRS = 2
N_SHIFT_MIX = 6
D_FF = ((8 * D_MODEL + 3 * 256 - 1) // (3 * 256)) * 256
N_FOURIER_LAYERS = (DEPTH + 1) // 2
N_RWKV_LAYERS = DEPTH // 2
RMS_EPS = 1e-6
GN_EPS = RWKV_HEAD_SIZE * 1e-5

kernel_name = "fnet_rwkv7_hybrid_encoder"


def rmsnorm(x, g):
    xf = x.astype(jnp.float32)
    y = xf * lax.rsqrt(jnp.mean(xf * xf, axis=-1, keepdims=True) + RMS_EPS)
    return (y * g.astype(jnp.float32)).astype(x.dtype)


def fourier_mix(h, w_out):
    b, s, d = h.shape
    hg = h.reshape(b, s, FOURIER_GROUPS, d // FOURIER_GROUPS).astype(jnp.float32)
    f = jnp.fft.fftn(hg, axes=(1, 3), norm="ortho").real
    return f.reshape(b, s, d).astype(h.dtype) @ w_out


def split_heads(t):
    return t.reshape(t.shape[:-1] + (t.shape[-1] // RWKV_HEAD_SIZE, RWKV_HEAD_SIZE))


def wkv7_scan(r, w, k, v, kk, a, reverse):
    b, s, hh, n = r.shape
    xs = tuple(jnp.moveaxis(t, 1, 0) for t in (r, w, k, v, kk, a))

    def step(state, inp):
        r_t, w_t, k_t, v_t, kk_t, a_t = inp
        sk = jnp.einsum('bhvk,bhk->bhv', state, kk_t)
        state = (state * w_t[:, :, None, :]
                 - sk[..., None] * (kk_t * a_t)[:, :, None, :]
                 + v_t[..., None] * k_t[:, :, None, :])
        y_t = jnp.einsum('bhvk,bhk->bhv', state, r_t)
        return state, y_t

    state0 = jnp.zeros((b, hh, n, n), jnp.float32)
    _, ys = lax.scan(step, state0, xs, reverse=reverse)
    return jnp.moveaxis(ys, 0, 1)


def rwkv7_mix(h, mu, w_rkv, w_o, w0, w1, w2, a0, a1, a2, g1, g2, k_k, k_a, r_k, ln_w, ln_b):
    f32 = jnp.float32
    b, s, d = h.shape
    hp = jnp.pad(h, ((0, 0), (1, 1), (0, 0)))
    xx = 0.5 * (hp[:, :-2] + hp[:, 2:]) - h
    xs = h[None] + xx[None] * mu[:, None, None, :]
    rkv = jnp.einsum('cbsd,cde->cbse', xs[:3], w_rkv)
    r, k, v = rkv[0], rkv[1], rkv[2]
    xw, xa, xg = xs[3], xs[4], xs[5]
    w_lora = jnp.einsum('jbsr,jre->jbse', jnp.tanh(jnp.einsum('bsd,jdr->jbsr', xw, w1)), w2)
    w_log = -jax.nn.softplus(-(w0[:, None, None, :] + w_lora).astype(f32)) - 0.5
    decay = jnp.exp(-jnp.exp(w_log))
    a_lora = jnp.einsum('jbsr,jre->jbse', jnp.einsum('bsd,jdr->jbsr', xa, a1), a2)
    a = jax.nn.sigmoid((a0[:, None, None, :] + a_lora).astype(f32))
    g = jax.nn.sigmoid(xg @ g1) @ g2
    rf = split_heads(r.astype(f32))
    vf = split_heads(v.astype(f32))
    kf = k.astype(f32)
    kk = split_heads(kf * k_k.astype(f32))
    kk = kk * lax.rsqrt(jnp.maximum(jnp.sum(kk * kk, axis=-1, keepdims=True), 1e-24))
    k_dir = split_heads(kf[None] * (1.0 + (a - 1.0) * k_a.astype(f32)))
    a_h = split_heads(a)
    decay_h = split_heads(decay)
    y = (wkv7_scan(rf, decay_h[0], k_dir[0], vf, kk, a_h[0], False)
         + wkv7_scan(rf, decay_h[1], k_dir[1], vf, kk, a_h[1], True))
    mean = jnp.mean(y, axis=-1, keepdims=True)
    var = jnp.mean(jnp.square(y - mean), axis=-1, keepdims=True)
    yn = ((y - mean) * lax.rsqrt(var + GN_EPS)).reshape(b, s, d) * ln_w.astype(f32) + ln_b.astype(f32)
    bonus = jnp.sum(jnp.sum(rf[None] * k_dir * r_k.astype(f32), axis=-1, keepdims=True), axis=0) * vf
    out = (yn + bonus.reshape(b, s, d)).astype(h.dtype) * g
    return out @ w_o


def swiglu(h, w_gate, w_up, w_down):
    return (jax.nn.silu(h @ w_gate) * (h @ w_up)) @ w_down


def setup_inputs(seed: int = 0) -> dict:
    key = jax.random.key(seed)
    ks = jax.random.split(key, 26)
    f32 = jnp.float32
    D, F, NF, NR = D_MODEL, D_FF, N_FOURIER_LAYERS, N_RWKV_LAYERS

    def nrm(k, shape, scale):
        return jax.random.normal(k, shape, f32) * scale

    return {
        "x": nrm(ks[0], (BATCH, SEQ, D), 1.0),
        "norm_mix_g": 1.0 + nrm(ks[1], (DEPTH, D), 0.05),
        "norm_ffn_g": 1.0 + nrm(ks[2], (DEPTH, D), 0.05),
        "norm_final_g": 1.0 + nrm(ks[3], (D,), 0.05),
        "fno_w_out": nrm(ks[4], (NF, D, D), D ** -0.5),
        "rwkv_mu": jax.random.uniform(ks[5], (NR, N_SHIFT_MIX, D), f32),
        "rwkv_w_rkv": nrm(ks[6], (NR, 3, D, D), D ** -0.5),
        "rwkv_w_o": nrm(ks[7], (NR, D, D), D ** -0.5),
        "rwkv_w0": jax.random.uniform(ks[8], (NR, N_DIRS, D), f32, -4.0, 1.0),
        "rwkv_w1": nrm(ks[9], (NR, N_DIRS, D, DECAY_LORA), D ** -0.5),
        "rwkv_w2": nrm(ks[10], (NR, N_DIRS, DECAY_LORA, D), 0.3 * DECAY_LORA ** -0.5),
        "rwkv_a0": nrm(ks[11], (NR, N_DIRS, D), 0.3),
        "rwkv_a1": nrm(ks[12], (NR, N_DIRS, D, AAA_LORA), D ** -0.5),
        "rwkv_a2": nrm(ks[13], (NR, N_DIRS, AAA_LORA, D), 0.3 * AAA_LORA ** -0.5),
        "rwkv_g1": nrm(ks[14], (NR, D, GATE_LORA), D ** -0.5),
        "rwkv_g2": nrm(ks[15], (NR, GATE_LORA, D), GATE_LORA ** -0.5),
        "rwkv_k_k": 0.85 + nrm(ks[16], (NR, D), 0.05),
        "rwkv_k_a": 1.0 + nrm(ks[17], (NR, D), 0.05),
        "rwkv_r_k": nrm(ks[18], (NR, RWKV_HEADS, RWKV_HEAD_SIZE), 0.1),
        "rwkv_ln_w": 1.0 + nrm(ks[19], (NR, D), 0.05),
        "rwkv_ln_b": nrm(ks[20], (NR, D), 0.01),
        "ffn_w_gate": nrm(ks[21], (DEPTH, D, F), D ** -0.5),
        "ffn_w_up": nrm(ks[22], (DEPTH, D, F), D ** -0.5),
        "ffn_w_down": nrm(ks[23], (DEPTH, F, D), F ** -0.5),
    }


def reference(x, norm_mix_g, norm_ffn_g, norm_final_g, fno_w_out, rwkv_mu, rwkv_w_rkv, rwkv_w_o,
              rwkv_w0, rwkv_w1, rwkv_w2, rwkv_a0, rwkv_a1, rwkv_a2, rwkv_g1, rwkv_g2,
              rwkv_k_k, rwkv_k_a, rwkv_r_k, rwkv_ln_w, rwkv_ln_b, ffn_w_gate, ffn_w_up, ffn_w_down):
    for i in range(DEPTH):
        h = rmsnorm(x, norm_mix_g[i])
        j = i // N_MIXERS
        if i % N_MIXERS == 0:
            x = x + fourier_mix(h, fno_w_out[j])
        else:
            x = x + rwkv7_mix(h, rwkv_mu[j], rwkv_w_rkv[j], rwkv_w_o[j],
                              rwkv_w0[j], rwkv_w1[j], rwkv_w2[j],
                              rwkv_a0[j], rwkv_a1[j], rwkv_a2[j],
                              rwkv_g1[j], rwkv_g2[j], rwkv_k_k[j], rwkv_k_a[j],
                              rwkv_r_k[j], rwkv_ln_w[j], rwkv_ln_b[j])
        h = rmsnorm(x, norm_ffn_g[i])
        x = x + swiglu(h, ffn_w_gate[i], ffn_w_up[i], ffn_w_down[i])
    return rmsnorm(x, norm_final_g)
```

```python
import functools
import math

import jax
import jax.numpy as jnp
from jax import lax
from jax.experimental import pallas as pl
from jax.experimental.pallas import tpu as pltpu

F32 = jnp.float32
BF16 = jnp.bfloat16

D_MODEL = 1024
HEAD = 64
GROUP_HEADS = 4
GROUP_W = GROUP_HEADS * HEAD
N_GROUPS = D_MODEL // GROUP_W
CHUNK = 64
FOURIER_GROUPS = 4
FOURIER_GW = D_MODEL // FOURIER_GROUPS
RMS_EPS = 1e-6
GN_EPS = HEAD * 1e-5
DECAY_SCALE = math.exp(-0.5)
VMEM_LIMIT = 56 * 1024 * 1024


def _params():
    return pltpu.CompilerParams(vmem_limit_bytes=VMEM_LIMIT)


def _dot(a, b):
    return jnp.dot(a, b, preferred_element_type=F32)


def _dot_nt(a, b):
    return lax.dot_general(a, b, (((1,), (1,)), ((), ())), preferred_element_type=F32)


def _dot_tn(a, b):
    return lax.dot_general(a, b, (((0,), (0,)), ((), ())), preferred_element_type=F32)


def _rms(x, g):
    return x * lax.rsqrt(jnp.mean(x * x, axis=-1, keepdims=True) + RMS_EPS) * g


def _sigmoid(z):
    return 1.0 / (1.0 + jnp.exp(-z))


def _head_sums(u, ones_bd):
    outs = []
    for g in range(N_GROUPS):
        ug = u[:, g * GROUP_W:(g + 1) * GROUP_W]
        hi = ug.astype(BF16)
        lo = (ug - hi.astype(F32)).astype(BF16)
        outs.append(_dot(hi, ones_bd) + _dot(lo, ones_bd))
    return jnp.concatenate(outs, axis=1)


def _fourier1_kernel(x_ref, g_ref, ccsc_ref, z_ref):
    h = _rms(x_ref[...], g_ref[...]).astype(BF16)
    for g in range(FOURIER_GROUPS):
        cols = slice(g * FOURIER_GW, (g + 1) * FOURIER_GW)
        z = _dot(h[:, cols], ccsc_ref[...])
        z_ref[0, 0, :, cols] = z[:, :FOURIER_GW].astype(BF16)
        z_ref[0, 1, :, cols] = z[:, FOURIER_GW:].astype(BF16)


def _fourier2_kernel(cs_ref, z_ref, w_ref, x_ref, o_ref):
    f = _dot(cs_ref[...], z_ref[0])
    o_ref[...] = x_ref[...] + _dot(f.astype(BF16), w_ref[...])


def _fourier_mix(x2d, g, w_out, batch, seq):
    t, d = x2d.shape
    tm = 512
    n = jnp.arange(FOURIER_GW, dtype=jnp.int32)
    ang = ((n[:, None] * n[None, :]) % FOURIER_GW).astype(F32) * (2.0 * math.pi / FOURIER_GW)
    ccsc = (jnp.concatenate([jnp.cos(ang), jnp.sin(ang)], axis=1) * (FOURIER_GW ** -0.5)).astype(BF16)
    n = jnp.arange(seq, dtype=jnp.int32)
    ang = ((n[:, None] * n[None, :]) % seq).astype(F32) * (2.0 * math.pi / seq)
    cs = (jnp.concatenate([jnp.cos(ang), -jnp.sin(ang)], axis=1) * (seq ** -0.5)).astype(BF16)

    spt = seq // tm
    z = pl.pallas_call(
        _fourier1_kernel,
        grid=(t // tm,),
        in_specs=[pl.BlockSpec((tm, d), lambda i: (i, 0)),
                  pl.BlockSpec((1, d), lambda i: (0, 0)),
                  pl.BlockSpec((FOURIER_GW, 2 * FOURIER_GW), lambda i: (0, 0))],
        out_specs=pl.BlockSpec((1, 2, tm, d), lambda i: (i // spt, 0, i % spt, 0)),
        out_shape=jax.ShapeDtypeStruct((batch, 2, seq, d), BF16),
        compiler_params=_params(),
        name="fourier_channel_dft",
    )(x2d, g.reshape(1, d), ccsc)
    z = z.reshape(batch, 2 * seq, d)

    return pl.pallas_call(
        _fourier2_kernel,
        grid=(batch, spt),
        in_specs=[pl.BlockSpec((tm, 2 * seq), lambda b, i: (i, 0)),
                  pl.BlockSpec((1, 2 * seq, d), lambda b, i: (b, 0, 0)),
                  pl.BlockSpec((d, d), lambda b, i: (0, 0)),
                  pl.BlockSpec((tm, d), lambda b, i: (b * spt + i, 0))],
        out_specs=pl.BlockSpec((tm, d), lambda b, i: (b * spt + i, 0)),
        out_shape=jax.ShapeDtypeStruct((t, d), F32),
        compiler_params=_params(),
        name="fourier_seq_dft_out",
    )(cs, z, w_out.astype(BF16), x2d)


def _ffn_kernel(x_ref, g_ref, wg_ref, wu_ref, wd_ref, gf_ref, o_ref, *, final_norm):
    x = x_ref[...]
    h = _rms(x, g_ref[...]).astype(BF16)
    gate = _dot(h, wg_ref[...])
    up = _dot(h, wu_ref[...])
    act = (gate * _sigmoid(gate) * up).astype(BF16)
    y = x + _dot(act, wd_ref[...])
    if final_norm:
        y = _rms(y, gf_ref[...])
    o_ref[...] = y


def _ffn(x2d, g, w_gate, w_up, w_down, g_final, final_norm):
    t, d = x2d.shape
    f = w_gate.shape[1]
    tm = 256
    return pl.pallas_call(
        functools.partial(_ffn_kernel, final_norm=final_norm),
        grid=(t // tm,),
        in_specs=[pl.BlockSpec((tm, d), lambda i: (i, 0)),
                  pl.BlockSpec((1, d), lambda i: (0, 0)),
                  pl.BlockSpec((d, f), lambda i: (0, 0)),
                  pl.BlockSpec((d, f), lambda i: (0, 0)),
                  pl.BlockSpec((f, d), lambda i: (0, 0)),
                  pl.BlockSpec((1, d), lambda i: (0, 0))],
        out_specs=pl.BlockSpec((tm, d), lambda i: (i, 0)),
        out_shape=jax.ShapeDtypeStruct((t, d), F32),
        compiler_params=_params(),
        name="swiglu_ffn",
    )(x2d, g.reshape(1, d), w_gate.astype(BF16), w_up.astype(BF16), w_down.astype(BF16),
      g_final.reshape(1, d))


def _rwkv_prep_kernel(x_ref, xp_ref, xn_ref, g_ref, mu_ref, wrkv_ref, w1_ref, w2_ref, a1_ref, a2_ref,
                      g1_ref, g2_ref, w0_ref, a0_ref, kk_ref, ka_ref, ones_ref,
                      r_o, v_o, kkn_o, gate_o, kd0_o, kd1_o, b0_o, b1_o, lw0_o, lw1_o, *, tiles_per_seq):
    i = pl.program_id(0)
    tm = x_ref.shape[0]
    g = g_ref[...]
    h = _rms(x_ref[...], g)
    pos = i % tiles_per_seq
    keep_prev = jnp.where(pos == 0, 0.0, 1.0)
    keep_next = jnp.where(pos == tiles_per_seq - 1, 0.0, 1.0)
    h_prev = _rms(xp_ref[...], g)[7:8, :] * keep_prev
    h_next = _rms(xn_ref[...], g)[0:1, :] * keep_next
    row = lax.broadcasted_iota(jnp.int32, h.shape, 0)
    h_dn = jnp.where(row == 0, h_prev, pltpu.roll(h, 1, 0))
    h_up = jnp.where(row == tm - 1, h_next, pltpu.roll(h, tm - 1, 0))
    xx = 0.5 * (h_dn + h_up) - h

    def mix(c):
        return (h + xx * mu_ref[c:c + 1, :]).astype(BF16)

    r = _dot(mix(0), wrkv_ref[0])
    k = _dot(mix(1), wrkv_ref[1])
    v = _dot(mix(2), wrkv_ref[2])
    w_lo = jnp.tanh(_dot(mix(3), w1_ref[...])).astype(BF16)
    a_lo = _dot(mix(4), a1_ref[...]).astype(BF16)
    g_lo = _sigmoid(_dot(mix(5), g1_ref[...])).astype(BF16)
    gate_o[...] = _dot(g_lo, g2_ref[...]).astype(BF16)
    r_o[...] = r.astype(BF16)
    v_o[...] = v.astype(BF16)

    kkr = k * kk_ref[...]
    ss = _head_sums(kkr * kkr, ones_ref[...])
    kkn = kkr * lax.rsqrt(jnp.maximum(ss, 1e-24))
    kkn_o[...] = kkn.astype(BF16)
    ka = ka_ref[...]
    for j, (kd_o, b_o, lw_o) in enumerate(((kd0_o, b0_o, lw0_o), (kd1_o, b1_o, lw1_o))):
        zw = w0_ref[j:j + 1, :] + _dot(w_lo, w2_ref[j])
        lw_o[...] = -DECAY_SCALE * _sigmoid(zw)
        a = _sigmoid(a0_ref[j:j + 1, :] + _dot(a_lo, a2_ref[j]))
        kd_o[...] = (k * (1.0 + (a - 1.0) * ka)).astype(BF16)
        b_o[...] = (kkn * a).astype(BF16)


def _wkv_group(r, v, kk, kd, b, e_pos, e_neg, e_excl, e_last, w_last, state, c):
    grp = c["grp"]

    def bd(y):
        yb = y.astype(BF16)
        zero = jnp.zeros_like(yb)
        return jnp.concatenate([jnp.where(grp == p, yb, zero) for p in range(GROUP_HEADS)], axis=0)

    def unbd(z):
        out = jnp.where(grp == 0, z[0:HEAD], 0.0)
        for p in range(1, GROUP_HEADS):
            out = out + jnp.where(grp == p, z[p * HEAD:(p + 1) * HEAD], 0.0)
        return out

    def mm(x, wbd):
        return _dot(x.astype(BF16), wbd)

    rd = r * e_pos
    kkd = kk * e_excl
    kdiv = kd * e_neg
    bdiv = b * e_neg
    kw = kd * e_last
    bw = b * e_last
    x = jnp.concatenate([kkd, rd], axis=0).astype(BF16)
    sk = _dot_nt(x, bd(kdiv))
    sb = _dot_nt(x, bd(bdiv))
    m_kk = jnp.where(c["strict"], sk[:CHUNK], 0.0)
    a_rk = jnp.where(c["incl"], sk[CHUNK:], 0.0)
    m_kb = jnp.where(c["strict"], sb[:CHUNK], 0.0)
    a_rb = jnp.where(c["incl"], sb[CHUNK:], 0.0)
    a = -m_kb
    tinv = jnp.where(c["eye"], 1.0, 0.0) + a
    ap = mm(a, bd(a))
    for j in range(5):
        wj = bd(ap)
        if j < 4:
            both = mm(jnp.concatenate([tinv, ap], axis=0), wj)
            tinv = tinv + both[:CHUNK]
            ap = both[CHUNK:]
        else:
            tinv = tinv + mm(tinv, wj)
    both = mm(jnp.concatenate([m_kk, a_rk], axis=0), bd(v))
    mv = both[:CHUNK]
    p_ = mm(tinv, bd(kkd))
    q_ = mm(tinv, bd(mv))
    y0 = both[CHUNK:] - mm(a_rb, bd(q_))
    rp = rd - mm(a_rb, bd(p_))
    h_ = unbd(_dot_tn(jnp.concatenate([kw, -bw], axis=0).astype(BF16),
                      jnp.concatenate([v, q_], axis=0).astype(BF16)))
    g_ = jnp.where(c["eye"], w_last, 0.0) - unbd(_dot_tn(bw.astype(BF16), p_.astype(BF16)))
    both = mm(jnp.concatenate([g_, rp], axis=0), bd(state))
    return both[CHUNK:] + y0, both[:CHUNK] + h_


def _wkv_consts(reverse):
    row = lax.broadcasted_iota(jnp.int32, (CHUNK, GROUP_W), 0)
    lane = lax.broadcasted_iota(jnp.int32, (CHUNK, GROUP_W), 1)
    s = lane % HEAD
    ri = lax.broadcasted_iota(jnp.int32, (CHUNK, CHUNK), 0)
    ci = lax.broadcasted_iota(jnp.int32, (CHUNK, CHUNK), 1)
    if reverse:
        strict, incl, tri = s > row, s >= row, ci >= ri
    else:
        strict, incl, tri = s < row, s <= row, ci <= ri
    return dict(strict=strict, incl=incl, eye=s == row, grp=lane // HEAD,
                tri=jnp.where(tri, 1.0, 0.0).astype(BF16))


def _wkv_kernel(rf, vf, kkf, kdf, bf, lwf, rb, vb, kkb, kdb, bb, lwb, yf_o, yb_o, state):
    @pl.when(pl.program_id(1) == 0)
    def _():
        state[...] = jnp.zeros_like(state)

    for d, (r_ref, v_ref, kk_ref, kd_ref, b_ref, lw_ref, y_o) in enumerate(
            ((rf, vf, kkf, kdf, bf, lwf, yf_o), (rb, vb, kkb, kdb, bb, lwb, yb_o))):
        reverse = d == 1
        c = _wkv_consts(reverse)
        lw = lw_ref[...]
        hi = lw.astype(BF16)
        lo = (lw - hi.astype(F32)).astype(BF16)
        cum = _dot(c["tri"], hi) + _dot(c["tri"], lo)
        last = cum[0:1, :] if reverse else cum[CHUNK - 1:CHUNK, :]
        e_pos = jnp.exp(cum)
        e_neg = jnp.exp(-cum)
        e_excl = jnp.exp(cum - lw)
        e_last = jnp.exp(last - cum)
        w_last = jnp.exp(last)
        for g in range(N_GROUPS):
            cols = slice(g * GROUP_W, (g + 1) * GROUP_W)
            y, s_new = _wkv_group(
                r_ref[:, cols].astype(F32), v_ref[:, cols].astype(F32), kk_ref[:, cols].astype(F32),
                kd_ref[:, cols].astype(F32), b_ref[:, cols].astype(F32),
                e_pos[:, cols], e_neg[:, cols], e_excl[:, cols], e_last[:, cols], w_last[:, cols],
                state[d, g], c)
            state[d, g] = s_new
            y_o[:, cols] = y


def _rwkv_post_kernel(yf_ref, yb_ref, r_ref, v_ref, kd0_ref, kd1_ref, gate_ref, x_ref,
                      lnw_ref, lnb_ref, rk_ref, wo_ref, ones_ref, o_ref):
    ones_bd = ones_ref[...]
    y = yf_ref[...] + yb_ref[...]
    mean = _head_sums(y, ones_bd) * (1.0 / HEAD)
    dlt = y - mean
    var = _head_sums(dlt * dlt, ones_bd) * (1.0 / HEAD)
    yn = dlt * lax.rsqrt(var + GN_EPS) * lnw_ref[...] + lnb_ref[...]
    r = r_ref[...].astype(F32)
    ksum = kd0_ref[...].astype(F32) + kd1_ref[...].astype(F32)
    bonus = _head_sums(r * ksum * rk_ref[...], ones_bd) * v_ref[...].astype(F32)
    out = ((yn + bonus) * gate_ref[...].astype(F32)).astype(BF16)
    o_ref[...] = x_ref[...] + _dot(out, wo_ref[...])


def _rwkv_mix(x2d, g, mu, w_rkv, w_o, w0, w1, w2, a0, a1, a2, g1, g2, k_k, k_a, r_k, ln_w, ln_b,
              batch, seq):
    t, d = x2d.shape
    tm = 256
    tiles_per_seq = seq // tm
    lora = w1.shape[-1]
    n_dirs = w1.shape[0]
    row = lambda vec: vec.reshape(1, d)

    w1cat = jnp.concatenate([w1[j] for j in range(n_dirs)], axis=1).astype(BF16)
    a1cat = jnp.concatenate([a1[j] for j in range(n_dirs)], axis=1).astype(BF16)
    w2pad = jnp.stack([jnp.pad(w2[j], ((j * lora, (n_dirs - 1 - j) * lora), (0, 0))) for j in range(n_dirs)]).astype(BF16)
    a2pad = jnp.stack([jnp.pad(a2[j], ((j * lora, (n_dirs - 1 - j) * lora), (0, 0))) for j in range(n_dirs)]).astype(BF16)
    li = jnp.arange(GROUP_W, dtype=jnp.int32) // HEAD
    ones_bd = (li[:, None] == li[None, :]).astype(BF16)

    full = lambda a: pl.BlockSpec(a.shape, lambda i: (0,) * a.ndim)
    tile = pl.BlockSpec((tm, d), lambda i: (i, 0))
    n8 = t // 8
    prev8 = pl.BlockSpec((8, d), lambda i: (jnp.maximum(i * (tm // 8) - 1, 0), 0))
    next8 = pl.BlockSpec((8, d), lambda i: (jnp.minimum((i + 1) * (tm // 8), n8 - 1), 0))
    consts = (row(g), mu, w_rkv.astype(BF16), w1cat, w2pad, a1cat, a2pad, g1.astype(BF16), g2.astype(BF16),
              w0, a0, row(k_k), row(k_a), ones_bd)
    bf_out = jax.ShapeDtypeStruct((t, d), BF16)
    f32_out = jax.ShapeDtypeStruct((t, d), F32)
    r, v, kkn, gate, kd0, kd1, b0, b1, lw0, lw1 = pl.pallas_call(
        functools.partial(_rwkv_prep_kernel, tiles_per_seq=tiles_per_seq),
        grid=(t // tm,),
        in_specs=[tile, prev8, next8] + [full(a) for a in consts],
        out_specs=[tile] * 10,
        out_shape=[bf_out] * 8 + [f32_out] * 2,
        compiler_params=_params(),
        name="rwkv_prep",
    )(x2d, x2d, x2d, *consts)

    nc = seq // CHUNK
    fwd = pl.BlockSpec((CHUNK, d), lambda b, c: (b * nc + c, 0))
    bwd = pl.BlockSpec((CHUNK, d), lambda b, c: (b * nc + nc - 1 - c, 0))
    yf, yb = pl.pallas_call(
        _wkv_kernel,
        grid=(batch, nc),
        in_specs=[fwd] * 6 + [bwd] * 6,
        out_specs=[fwd, bwd],
        out_shape=[f32_out, f32_out],
        scratch_shapes=[pltpu.VMEM((2, N_GROUPS, HEAD, GROUP_W), F32)],
        compiler_params=_params(),
        name="wkv7_chunked",
    )(r, v, kkn, kd0, b0, lw0, r, v, kkn, kd1, b1, lw1)

    consts = (row(ln_w), row(ln_b), row(r_k), w_o.astype(BF16), ones_bd)
    return pl.pallas_call(
        _rwkv_post_kernel,
        grid=(t // tm,),
        in_specs=[tile] * 8 + [full(a) for a in consts],
        out_specs=tile,
        out_shape=f32_out,
        compiler_params=_params(),
        name="rwkv_post",
    )(yf, yb, r, v, kd0, kd1, gate, x2d, *consts)


def kernel(x, norm_mix_g, norm_ffn_g, norm_final_g, fno_w_out, rwkv_mu, rwkv_w_rkv, rwkv_w_o, rwkv_w0, rwkv_w1, rwkv_w2, rwkv_a0, rwkv_a1, rwkv_a2, rwkv_g1, rwkv_g2, rwkv_k_k, rwkv_k_a, rwkv_r_k, rwkv_ln_w, rwkv_ln_b, ffn_w_gate, ffn_w_up, ffn_w_down):
    batch, seq, d = x.shape
    depth = norm_mix_g.shape[0]
    h = x.reshape(batch * seq, d)
    for i in range(depth):
        j = i // 2
        if i % 2 == 0:
            h = _fourier_mix(h, norm_mix_g[i], fno_w_out[j], batch, seq)
        else:
            h = _rwkv_mix(h, norm_mix_g[i], rwkv_mu[j], rwkv_w_rkv[j], rwkv_w_o[j], rwkv_w0[j], rwkv_w1[j],
                          rwkv_w2[j], rwkv_a0[j], rwkv_a1[j], rwkv_a2[j], rwkv_g1[j], rwkv_g2[j],
                          rwkv_k_k[j], rwkv_k_a[j], rwkv_r_k[j], rwkv_ln_w[j], rwkv_ln_b[j], batch, seq)
        h = _ffn(h, norm_ffn_g[i], ffn_w_gate[i], ffn_w_up[i], ffn_w_down[i], norm_final_g,
                 final_norm=(i == depth - 1))
    return h.reshape(batch, seq, d)
```

```python
import functools
import math

import jax
import jax.numpy as jnp
from jax import lax
from jax.experimental import pallas as pl
from jax.experimental.pallas import tpu as pltpu

F32 = jnp.float32
BF16 = jnp.bfloat16

D_MODEL = 1024
HEAD = 64
GROUP_HEADS = 4
GROUP_W = GROUP_HEADS * HEAD
N_GROUPS = D_MODEL // GROUP_W
CHUNK = 64
FOURIER_GROUPS = 4
FOURIER_GW = D_MODEL // FOURIER_GROUPS
RMS_EPS = 1e-6
GN_EPS = HEAD * 1e-5
DECAY_SCALE = math.exp(-0.5)
VMEM_LIMIT = 56 * 1024 * 1024


def _params():
    return pltpu.CompilerParams(vmem_limit_bytes=VMEM_LIMIT)


def _dot(a, b):
    return jnp.dot(a, b, preferred_element_type=F32)


def _dot_nt(a, b):
    return lax.dot_general(a, b, (((1,), (1,)), ((), ())), preferred_element_type=F32)


def _dot_tn(a, b):
    return lax.dot_general(a, b, (((0,), (0,)), ((), ())), preferred_element_type=F32)


def _rms(x, g):
    return x * lax.rsqrt(jnp.mean(x * x, axis=-1, keepdims=True) + RMS_EPS) * g


def _sigmoid(z):
    return 1.0 / (1.0 + jnp.exp(-z))


def _head_sums(u, ones_bd):
    outs = []
    for g in range(N_GROUPS):
        ug = u[:, g * GROUP_W:(g + 1) * GROUP_W]
        hi = ug.astype(BF16)
        lo = (ug - hi.astype(F32)).astype(BF16)
        outs.append(_dot(hi, ones_bd) + _dot(lo, ones_bd))
    return jnp.concatenate(outs, axis=1)


def _fourier1_kernel(x_ref, g_ref, ccsc_ref, z_ref):
    h = _rms(x_ref[...], g_ref[...]).astype(BF16)
    for g in range(FOURIER_GROUPS):
        cols = slice(g * FOURIER_GW, (g + 1) * FOURIER_GW)
        z = _dot(h[:, cols], ccsc_ref[...])
        z_ref[0, 0, :, cols] = z[:, :FOURIER_GW].astype(BF16)
        z_ref[0, 1, :, cols] = z[:, FOURIER_GW:].astype(BF16)


def _fourier2_kernel(cs_ref, z_ref, w_ref, x_ref, o_ref):
    f = _dot(cs_ref[...], z_ref[0])
    o_ref[...] = x_ref[...] + _dot(f.astype(BF16), w_ref[...])


def _fourier_mix(x2d, g, w_out, batch, seq):
    t, d = x2d.shape
    tm = 512
    n = jnp.arange(FOURIER_GW, dtype=jnp.int32)
    ang = ((n[:, None] * n[None, :]) % FOURIER_GW).astype(F32) * (2.0 * math.pi / FOURIER_GW)
    ccsc = (jnp.concatenate([jnp.cos(ang), jnp.sin(ang)], axis=1) * (FOURIER_GW ** -0.5)).astype(BF16)
    n = jnp.arange(seq, dtype=jnp.int32)
    ang = ((n[:, None] * n[None, :]) % seq).astype(F32) * (2.0 * math.pi / seq)
    cs = (jnp.concatenate([jnp.cos(ang), -jnp.sin(ang)], axis=1) * (seq ** -0.5)).astype(BF16)

    spt = seq // tm
    z = pl.pallas_call(
        _fourier1_kernel,
        grid=(t // tm,),
        in_specs=[pl.BlockSpec((tm, d), lambda i: (i, 0)),
                  pl.BlockSpec((1, d), lambda i: (0, 0)),
                  pl.BlockSpec((FOURIER_GW, 2 * FOURIER_GW), lambda i: (0, 0))],
        out_specs=pl.BlockSpec((1, 2, tm, d), lambda i: (i // spt, 0, i % spt, 0)),
        out_shape=jax.ShapeDtypeStruct((batch, 2, seq, d), BF16),
        compiler_params=_params(),
        name="fourier_channel_dft",
    )(x2d, g.reshape(1, d), ccsc)
    z = z.reshape(batch, 2 * seq, d)

    return pl.pallas_call(
        _fourier2_kernel,
        grid=(batch, spt),
        in_specs=[pl.BlockSpec((tm, 2 * seq), lambda b, i: (i, 0)),
                  pl.BlockSpec((1, 2 * seq, d), lambda b, i: (b, 0, 0)),
                  pl.BlockSpec((d, d), lambda b, i: (0, 0)),
                  pl.BlockSpec((tm, d), lambda b, i: (b * spt + i, 0))],
        out_specs=pl.BlockSpec((tm, d), lambda b, i: (b * spt + i, 0)),
        out_shape=jax.ShapeDtypeStruct((t, d), F32),
        compiler_params=_params(),
        name="fourier_seq_dft_out",
    )(cs, z, w_out.astype(BF16), x2d)


def _ffn_kernel(x_ref, g_ref, wg_ref, wu_ref, wd_ref, gf_ref, o_ref, *, final_norm):
    x = x_ref[...]
    h = _rms(x, g_ref[...]).astype(BF16)
    gate = _dot(h, wg_ref[...])
    up = _dot(h, wu_ref[...])
    act = (gate * _sigmoid(gate) * up).astype(BF16)
    y = x + _dot(act, wd_ref[...])
    if final_norm:
        y = _rms(y, gf_ref[...])
    o_ref[...] = y


def _ffn(x2d, g, w_gate, w_up, w_down, g_final, final_norm):
    t, d = x2d.shape
    f = w_gate.shape[1]
    tm = 256
    return pl.pallas_call(
        functools.partial(_ffn_kernel, final_norm=final_norm),
        grid=(t // tm,),
        in_specs=[pl.BlockSpec((tm, d), lambda i: (i, 0)),
                  pl.BlockSpec((1, d), lambda i: (0, 0)),
                  pl.BlockSpec((d, f), lambda i: (0, 0)),
                  pl.BlockSpec((d, f), lambda i: (0, 0)),
                  pl.BlockSpec((f, d), lambda i: (0, 0)),
                  pl.BlockSpec((1, d), lambda i: (0, 0))],
        out_specs=pl.BlockSpec((tm, d), lambda i: (i, 0)),
        out_shape=jax.ShapeDtypeStruct((t, d), F32),
        compiler_params=_params(),
        name="swiglu_ffn",
    )(x2d, g.reshape(1, d), w_gate.astype(BF16), w_up.astype(BF16), w_down.astype(BF16),
      g_final.reshape(1, d))


def _rwkv_prep_kernel(x_ref, xp_ref, xn_ref, g_ref, mu_ref, wrkv_ref, w1_ref, w2_ref, a1_ref, a2_ref,
                      g1_ref, g2_ref, w0_ref, a0_ref, kk_ref, ka_ref, ones_ref,
                      r_o, v_o, kkn_o, gate_o, kd0_o, kd1_o, b0_o, b1_o, lw0_o, lw1_o, *, tiles_per_seq):
    i = pl.program_id(0)
    tm = x_ref.shape[0]
    g = g_ref[...]
    h = _rms(x_ref[...], g)
    pos = i % tiles_per_seq
    keep_prev = jnp.where(pos == 0, 0.0, 1.0)
    keep_next = jnp.where(pos == tiles_per_seq - 1, 0.0, 1.0)
    h_prev = _rms(xp_ref[...], g)[7:8, :] * keep_prev
    h_next = _rms(xn_ref[...], g)[0:1, :] * keep_next
    row = lax.broadcasted_iota(jnp.int32, h.shape, 0)
    h_dn = jnp.where(row == 0, h_prev, pltpu.roll(h, 1, 0))
    h_up = jnp.where(row == tm - 1, h_next, pltpu.roll(h, tm - 1, 0))
    xx = 0.5 * (h_dn + h_up) - h

    def mix(c):
        return (h + xx * mu_ref[c:c + 1, :]).astype(BF16)

    r = _dot(mix(0), wrkv_ref[0])
    k = _dot(mix(1), wrkv_ref[1])
    v = _dot(mix(2), wrkv_ref[2])
    w_lo = jnp.tanh(_dot(mix(3), w1_ref[...])).astype(BF16)
    a_lo = _dot(mix(4), a1_ref[...]).astype(BF16)
    g_lo = _sigmoid(_dot(mix(5), g1_ref[...])).astype(BF16)
    gate_o[...] = _dot(g_lo, g2_ref[...]).astype(BF16)
    r_o[...] = r.astype(BF16)
    v_o[...] = v.astype(BF16)

    kkr = k * kk_ref[...]
    ss = _head_sums(kkr * kkr, ones_ref[...])
    kkn = kkr * lax.rsqrt(jnp.maximum(ss, 1e-24))
    kkn_o[...] = kkn.astype(BF16)
    ka = ka_ref[...]
    for j, (kd_o, b_o, lw_o) in enumerate(((kd0_o, b0_o, lw0_o), (kd1_o, b1_o, lw1_o))):
        zw = w0_ref[j:j + 1, :] + _dot(w_lo, w2_ref[j])
        lw_o[...] = -DECAY_SCALE * _sigmoid(zw)
        a = _sigmoid(a0_ref[j:j + 1, :] + _dot(a_lo, a2_ref[j]))
        kd_o[...] = (k * (1.0 + (a - 1.0) * ka)).astype(BF16)
        b_o[...] = (kkn * a).astype(BF16)


def _wkv_groups(ops, consts):
    n = range(len(ops))
    grp = consts[0]["grp"]

    def bd(y):
        yb = y.astype(BF16)
        zero = jnp.zeros_like(yb)
        return jnp.concatenate([jnp.where(grp == p, yb, zero) for p in range(GROUP_HEADS)], axis=0)

    def tr(y):
        yt = y.T
        return jnp.concatenate([yt[p * HEAD:(p + 1) * HEAD] for p in range(GROUP_HEADS)], axis=1)

    def mm(x, y):
        return _dot(x.astype(BF16), bd(y))

    def cat(*xs):
        return jnp.concatenate(xs, axis=0)

    r, v, kk, kd, b, e_pos, e_neg, e_excl, e_last, w_last, state = zip(*ops)
    rd = [r[i] * e_pos[i] for i in n]
    kkd = [kk[i] * e_excl[i] for i in n]
    kdiv_t = [tr(kd[i] * e_neg[i]) for i in n]
    bdiv_t = [tr(b[i] * e_neg[i]) for i in n]
    kw_t = [tr(kd[i] * e_last[i]) for i in n]
    bw_t = [tr(b[i] * e_last[i]) for i in n]
    x = [cat(kkd[i], rd[i]).astype(BF16) for i in n]
    sk = [mm(x[i], kdiv_t[i]) for i in n]
    sb = [mm(x[i], bdiv_t[i]) for i in n]
    m_kk = [jnp.where(consts[i]["strict"], sk[i][:CHUNK], 0.0) for i in n]
    a_rk = [jnp.where(consts[i]["incl"], sk[i][CHUNK:], 0.0) for i in n]
    a_rb = [jnp.where(consts[i]["incl"], sb[i][CHUNK:], 0.0) for i in n]
    a = [jnp.where(consts[i]["strict"], -sb[i][:CHUNK], 0.0) for i in n]
    tinv = [jnp.where(consts[i]["eye"], 1.0, 0.0) + a[i] for i in n]
    ap = [mm(a[i], a[i]) for i in n]
    for j in range(5):
        if j < 4:
            both = [mm(cat(tinv[i], ap[i]), ap[i]) for i in n]
            tinv = [tinv[i] + both[i][:CHUNK] for i in n]
            ap = [both[i][CHUNK:] for i in n]
        else:
            tinv = [tinv[i] + mm(tinv[i], ap[i]) for i in n]
    xv = [mm(cat(m_kk[i], a_rk[i], kw_t[i]), v[i]) for i in n]
    uz = [mm(cat(a_rb[i], bw_t[i]), tinv[i]) for i in n]
    uzp = [mm(uz[i], kkd[i]) for i in n]
    uzq = [mm(uz[i], xv[i][:CHUNK]) for i in n]
    rp = [rd[i] - uzp[i][:CHUNK] for i in n]
    g_ = [jnp.where(consts[i]["eye"], w_last[i], 0.0) - uzp[i][CHUNK:] for i in n]
    y0 = [xv[i][CHUNK:2 * CHUNK] - uzq[i][:CHUNK] for i in n]
    h_ = [xv[i][2 * CHUNK:] - uzq[i][CHUNK:] for i in n]
    both = [mm(cat(g_[i], rp[i]), state[i]) for i in n]
    return [(both[i][CHUNK:] + y0[i], both[i][:CHUNK] + h_[i]) for i in n]


def _wkv_consts(reverse):
    row = lax.broadcasted_iota(jnp.int32, (CHUNK, GROUP_W), 0)
    lane = lax.broadcasted_iota(jnp.int32, (CHUNK, GROUP_W), 1)
    s = lane % HEAD
    ri = lax.broadcasted_iota(jnp.int32, (CHUNK, CHUNK), 0)
    ci = lax.broadcasted_iota(jnp.int32, (CHUNK, CHUNK), 1)
    if reverse:
        strict, incl, tri = s > row, s >= row, ci >= ri
    else:
        strict, incl, tri = s < row, s <= row, ci <= ri
    return dict(strict=strict, incl=incl, eye=s == row, grp=lane // HEAD,
                tri=jnp.where(tri, 1.0, 0.0).astype(BF16))


def _wkv_kernel(rf, vf, kkf, kdf, bf, lwf, rb, vb, kkb, kdb, bb, lwb, yf_o, yb_o, state):
    @pl.when(pl.program_id(1) == 0)
    def _():
        state[...] = jnp.zeros_like(state)

    ops, consts, outs = [], [], []
    for d, (r_ref, v_ref, kk_ref, kd_ref, b_ref, lw_ref, y_o) in enumerate(
            ((rf, vf, kkf, kdf, bf, lwf, yf_o), (rb, vb, kkb, kdb, bb, lwb, yb_o))):
        reverse = d == 1
        c = _wkv_consts(reverse)
        lw = lw_ref[...]
        hi = lw.astype(BF16)
        lo = (lw - hi.astype(F32)).astype(BF16)
        cum = _dot(c["tri"], hi) + _dot(c["tri"], lo)
        last = cum[0:1, :] if reverse else cum[CHUNK - 1:CHUNK, :]
        e_pos = jnp.exp(cum)
        e_neg = jnp.exp(-cum)
        e_excl = jnp.exp(cum - lw)
        e_last = jnp.exp(last - cum)
        w_last = jnp.exp(last)
        for g in range(N_GROUPS):
            cols = slice(g * GROUP_W, (g + 1) * GROUP_W)
            ops.append((r_ref[:, cols].astype(F32), v_ref[:, cols].astype(F32), kk_ref[:, cols].astype(F32),
                        kd_ref[:, cols].astype(F32), b_ref[:, cols].astype(F32),
                        e_pos[:, cols], e_neg[:, cols], e_excl[:, cols], e_last[:, cols], w_last[:, cols],
                        state[d, g]))
            consts.append(c)
            outs.append((y_o, cols, d, g))
    for (y, s_new), (y_o, cols, d, g) in zip(_wkv_groups(ops, consts), outs):
        state[d, g] = s_new
        y_o[:, cols] = y


def _rwkv_post_kernel(yf_ref, yb_ref, r_ref, v_ref, kd0_ref, kd1_ref, gate_ref, x_ref,
                      lnw_ref, lnb_ref, rk_ref, wo_ref, ones_ref, o_ref):
    ones_bd = ones_ref[...]
    y = yf_ref[...] + yb_ref[...]
    mean = _head_sums(y, ones_bd) * (1.0 / HEAD)
    dlt = y - mean
    var = _head_sums(dlt * dlt, ones_bd) * (1.0 / HEAD)
    yn = dlt * lax.rsqrt(var + GN_EPS) * lnw_ref[...] + lnb_ref[...]
    r = r_ref[...].astype(F32)
    ksum = kd0_ref[...].astype(F32) + kd1_ref[...].astype(F32)
    bonus = _head_sums(r * ksum * rk_ref[...], ones_bd) * v_ref[...].astype(F32)
    out = ((yn + bonus) * gate_ref[...].astype(F32)).astype(BF16)
    o_ref[...] = x_ref[...] + _dot(out, wo_ref[...])


def _rwkv_mix(x2d, g, mu, w_rkv, w_o, w0, w1, w2, a0, a1, a2, g1, g2, k_k, k_a, r_k, ln_w, ln_b,
              batch, seq):
    t, d = x2d.shape
    tm = 256
    tiles_per_seq = seq // tm
    lora = w1.shape[-1]
    n_dirs = w1.shape[0]
    row = lambda vec: vec.reshape(1, d)

    w1cat = jnp.concatenate([w1[j] for j in range(n_dirs)], axis=1).astype(BF16)
    a1cat = jnp.concatenate([a1[j] for j in range(n_dirs)], axis=1).astype(BF16)
    w2pad = jnp.stack([jnp.pad(w2[j], ((j * lora, (n_dirs - 1 - j) * lora), (0, 0))) for j in range(n_dirs)]).astype(BF16)
    a2pad = jnp.stack([jnp.pad(a2[j], ((j * lora, (n_dirs - 1 - j) * lora), (0, 0))) for j in range(n_dirs)]).astype(BF16)
    li = jnp.arange(GROUP_W, dtype=jnp.int32) // HEAD
    ones_bd = (li[:, None] == li[None, :]).astype(BF16)

    full = lambda a: pl.BlockSpec(a.shape, lambda i: (0,) * a.ndim)
    tile = pl.BlockSpec((tm, d), lambda i: (i, 0))
    n8 = t // 8
    prev8 = pl.BlockSpec((8, d), lambda i: (jnp.maximum(i * (tm // 8) - 1, 0), 0))
    next8 = pl.BlockSpec((8, d), lambda i: (jnp.minimum((i + 1) * (tm // 8), n8 - 1), 0))
    consts = (row(g), mu, w_rkv.astype(BF16), w1cat, w2pad, a1cat, a2pad, g1.astype(BF16), g2.astype(BF16),
              w0, a0, row(k_k), row(k_a), ones_bd)
    bf_out = jax.ShapeDtypeStruct((t, d), BF16)
    f32_out = jax.ShapeDtypeStruct((t, d), F32)
    r, v, kkn, gate, kd0, kd1, b0, b1, lw0, lw1 = pl.pallas_call(
        functools.partial(_rwkv_prep_kernel, tiles_per_seq=tiles_per_seq),
        grid=(t // tm,),
        in_specs=[tile, prev8, next8] + [full(a) for a in consts],
        out_specs=[tile] * 10,
        out_shape=[bf_out] * 8 + [f32_out] * 2,
        compiler_params=_params(),
        name="rwkv_prep",
    )(x2d, x2d, x2d, *consts)

    nc = seq // CHUNK
    fwd = pl.BlockSpec((CHUNK, d), lambda b, c: (b * nc + c, 0))
    bwd = pl.BlockSpec((CHUNK, d), lambda b, c: (b * nc + nc - 1 - c, 0))
    yf, yb = pl.pallas_call(
        _wkv_kernel,
        grid=(batch, nc),
        in_specs=[fwd] * 6 + [bwd] * 6,
        out_specs=[fwd, bwd],
        out_shape=[f32_out, f32_out],
        scratch_shapes=[pltpu.VMEM((2, N_GROUPS, HEAD, GROUP_W), F32)],
        compiler_params=_params(),
        name="wkv7_chunked",
    )(r, v, kkn, kd0, b0, lw0, r, v, kkn, kd1, b1, lw1)

    consts = (row(ln_w), row(ln_b), row(r_k), w_o.astype(BF16), ones_bd)
    return pl.pallas_call(
        _rwkv_post_kernel,
        grid=(t // tm,),
        in_specs=[tile] * 8 + [full(a) for a in consts],
        out_specs=tile,
        out_shape=f32_out,
        compiler_params=_params(),
        name="rwkv_post",
    )(yf, yb, r, v, kd0, kd1, gate, x2d, *consts)


def kernel(x, norm_mix_g, norm_ffn_g, norm_final_g, fno_w_out, rwkv_mu, rwkv_w_rkv, rwkv_w_o, rwkv_w0, rwkv_w1, rwkv_w2, rwkv_a0, rwkv_a1, rwkv_a2, rwkv_g1, rwkv_g2, rwkv_k_k, rwkv_k_a, rwkv_r_k, rwkv_ln_w, rwkv_ln_b, ffn_w_gate, ffn_w_up, ffn_w_down):
    batch, seq, d = x.shape
    depth = norm_mix_g.shape[0]
    h = x.reshape(batch * seq, d)
    for i in range(depth):
        j = i // 2
        if i % 2 == 0:
            h = _fourier_mix(h, norm_mix_g[i], fno_w_out[j], batch, seq)
        else:
            h = _rwkv_mix(h, norm_mix_g[i], rwkv_mu[j], rwkv_w_rkv[j], rwkv_w_o[j], rwkv_w0[j], rwkv_w1[j],
                          rwkv_w2[j], rwkv_a0[j], rwkv_a1[j], rwkv_a2[j], rwkv_g1[j], rwkv_g2[j],
                          rwkv_k_k[j], rwkv_k_a[j], rwkv_r_k[j], rwkv_ln_w[j], rwkv_ln_b[j], batch, seq)
        h = _ffn(h, norm_ffn_g[i], ffn_w_gate[i], ffn_w_up[i], ffn_w_down[i], norm_final_g,
                 final_norm=(i == depth - 1))
    return h.reshape(batch, seq, d)
```

```python
import functools
import math

import jax
import jax.numpy as jnp
from jax import lax
from jax.experimental import pallas as pl
from jax.experimental.pallas import tpu as pltpu

F32 = jnp.float32
BF16 = jnp.bfloat16

D_MODEL = 1024
HEAD = 64
GROUP_HEADS = 4
GROUP_W = GROUP_HEADS * HEAD
N_GROUPS = D_MODEL // GROUP_W
CHUNK = 64
FOURIER_GROUPS = 4
FOURIER_GW = D_MODEL // FOURIER_GROUPS
RMS_EPS = 1e-6
GN_EPS = HEAD * 1e-5
DECAY_SCALE = math.exp(-0.5)
VMEM_LIMIT = 56 * 1024 * 1024


def _params():
    return pltpu.CompilerParams(vmem_limit_bytes=VMEM_LIMIT)


def _dot(a, b):
    return jnp.dot(a, b, preferred_element_type=F32)


def _dot_nt(a, b):
    return lax.dot_general(a, b, (((1,), (1,)), ((), ())), preferred_element_type=F32)


def _dot_tn(a, b):
    return lax.dot_general(a, b, (((0,), (0,)), ((), ())), preferred_element_type=F32)


def _rms(x, g):
    return x * lax.rsqrt(jnp.mean(x * x, axis=-1, keepdims=True) + RMS_EPS) * g


def _sigmoid(z):
    return 1.0 / (1.0 + jnp.exp(-z))


def _head_sums(u, ones_bd):
    outs = []
    for g in range(N_GROUPS):
        ug = u[:, g * GROUP_W:(g + 1) * GROUP_W]
        hi = ug.astype(BF16)
        lo = (ug - hi.astype(F32)).astype(BF16)
        outs.append(_dot(hi, ones_bd) + _dot(lo, ones_bd))
    return jnp.concatenate(outs, axis=1)


def _fourier1_kernel(x_ref, g_ref, ccsc_ref, z_ref):
    h = _rms(x_ref[...], g_ref[...]).astype(BF16)
    for g in range(FOURIER_GROUPS):
        cols = slice(g * FOURIER_GW, (g + 1) * FOURIER_GW)
        z = _dot(h[:, cols], ccsc_ref[...])
        z_ref[0, 0, :, cols] = z[:, :FOURIER_GW].astype(BF16)
        z_ref[0, 1, :, cols] = z[:, FOURIER_GW:].astype(BF16)


def _fourier2_kernel(cs_ref, z_ref, w_ref, x_ref, o_ref):
    f = _dot(cs_ref[...], z_ref[0])
    o_ref[...] = x_ref[...] + _dot(f.astype(BF16), w_ref[...])


def _fourier_mix(x2d, g, w_out, batch, seq):
    t, d = x2d.shape
    tm = 512
    n = jnp.arange(FOURIER_GW, dtype=jnp.int32)
    ang = ((n[:, None] * n[None, :]) % FOURIER_GW).astype(F32) * (2.0 * math.pi / FOURIER_GW)
    ccsc = (jnp.concatenate([jnp.cos(ang), jnp.sin(ang)], axis=1) * (FOURIER_GW ** -0.5)).astype(BF16)
    n = jnp.arange(seq, dtype=jnp.int32)
    ang = ((n[:, None] * n[None, :]) % seq).astype(F32) * (2.0 * math.pi / seq)
    cs = (jnp.concatenate([jnp.cos(ang), -jnp.sin(ang)], axis=1) * (seq ** -0.5)).astype(BF16)

    spt = seq // tm
    z = pl.pallas_call(
        _fourier1_kernel,
        grid=(t // tm,),
        in_specs=[pl.BlockSpec((tm, d), lambda i: (i, 0)),
                  pl.BlockSpec((1, d), lambda i: (0, 0)),
                  pl.BlockSpec((FOURIER_GW, 2 * FOURIER_GW), lambda i: (0, 0))],
        out_specs=pl.BlockSpec((1, 2, tm, d), lambda i: (i // spt, 0, i % spt, 0)),
        out_shape=jax.ShapeDtypeStruct((batch, 2, seq, d), BF16),
        compiler_params=_params(),
        name="fourier_channel_dft",
    )(x2d, g.reshape(1, d), ccsc)
    z = z.reshape(batch, 2 * seq, d)

    return pl.pallas_call(
        _fourier2_kernel,
        grid=(batch, spt),
        in_specs=[pl.BlockSpec((tm, 2 * seq), lambda b, i: (i, 0)),
                  pl.BlockSpec((1, 2 * seq, d), lambda b, i: (b, 0, 0)),
                  pl.BlockSpec((d, d), lambda b, i: (0, 0)),
                  pl.BlockSpec((tm, d), lambda b, i: (b * spt + i, 0))],
        out_specs=pl.BlockSpec((tm, d), lambda b, i: (b * spt + i, 0)),
        out_shape=jax.ShapeDtypeStruct((t, d), F32),
        compiler_params=_params(),
        name="fourier_seq_dft_out",
    )(cs, z, w_out.astype(BF16), x2d)


def _ffn_kernel(x_ref, g_ref, wg_ref, wu_ref, wd_ref, gf_ref, o_ref, *, final_norm):
    x = x_ref[...]
    h = _rms(x, g_ref[...]).astype(BF16)
    gate = _dot(h, wg_ref[...])
    up = _dot(h, wu_ref[...])
    act = (gate * _sigmoid(gate) * up).astype(BF16)
    y = x + _dot(act, wd_ref[...])
    if final_norm:
        y = _rms(y, gf_ref[...])
    o_ref[...] = y


def _ffn(x2d, g, w_gate, w_up, w_down, g_final, final_norm):
    t, d = x2d.shape
    f = w_gate.shape[1]
    tm = 256
    return pl.pallas_call(
        functools.partial(_ffn_kernel, final_norm=final_norm),
        grid=(t // tm,),
        in_specs=[pl.BlockSpec((tm, d), lambda i: (i, 0)),
                  pl.BlockSpec((1, d), lambda i: (0, 0)),
                  pl.BlockSpec((d, f), lambda i: (0, 0)),
                  pl.BlockSpec((d, f), lambda i: (0, 0)),
                  pl.BlockSpec((f, d), lambda i: (0, 0)),
                  pl.BlockSpec((1, d), lambda i: (0, 0))],
        out_specs=pl.BlockSpec((tm, d), lambda i: (i, 0)),
        out_shape=jax.ShapeDtypeStruct((t, d), F32),
        compiler_params=_params(),
        name="swiglu_ffn",
    )(x2d, g.reshape(1, d), w_gate.astype(BF16), w_up.astype(BF16), w_down.astype(BF16),
      g_final.reshape(1, d))


def _rwkv_prep_kernel(x_ref, xp_ref, xn_ref, g_ref, mu_ref, sbar_ref, wrkv_ref, w1_ref, w2_ref, a1_ref, a2_ref,
                      g1_ref, g2_ref, w0_ref, a0_ref, kk_ref, ka_ref, ones_ref,
                      r_o, v_o, kkn_o, gate_o, kd0_o, kd1_o, b0_o, b1_o, lw0_o, lw1_o, *, tiles_per_seq):
    i = pl.program_id(0)
    tm = x_ref.shape[0]
    g = g_ref[...]
    h = _rms(x_ref[...], g)
    hb = h.astype(BF16)
    pos = i % tiles_per_seq
    keep_prev = jnp.where(pos == 0, 0.0, 0.5)
    keep_next = jnp.where(pos == tiles_per_seq - 1, 0.0, 0.5)
    h_prev = _rms(xp_ref[...], g)[7:8, :] * keep_prev
    h_next = _rms(xn_ref[...], g)[0:1, :] * keep_next
    hbar = _dot(sbar_ref[...], hb)
    row8 = lax.broadcasted_iota(jnp.int32, (8, h.shape[1]), 0)
    hbar = jnp.concatenate([hbar[:8] + jnp.where(row8 == 0, h_prev, 0.0), hbar[8:tm - 8],
                            hbar[tm - 8:] + jnp.where(row8 == 7, h_next, 0.0)], axis=0)
    xx = (hbar - h).astype(BF16)
    mu = mu_ref[...].astype(BF16)

    def mix(c):
        return hb + xx * mu[c:c + 1, :]

    xr, xk, xv = mix(0), mix(1), mix(2)
    w_lo = jnp.tanh(_dot(mix(3), w1_ref[...])).astype(BF16)
    a_lo = _dot(mix(4), a1_ref[...]).astype(BF16)
    g_lo = _sigmoid(_dot(mix(5), g1_ref[...])).astype(BF16)
    ones_bd = ones_ref[...]
    def cols_of(grp):
        return slice(grp * GROUP_W, (grp + 1) * GROUP_W)

    def dots(grp):
        cols = cols_of(grp)
        r_o[:, cols] = _dot(xr, wrkv_ref[0, :, cols]).astype(BF16)
        v_o[:, cols] = _dot(xv, wrkv_ref[2, :, cols]).astype(BF16)
        gate_o[:, cols] = _dot(g_lo, g2_ref[:, cols]).astype(BF16)
        return (_dot(xk, wrkv_ref[1, :, cols]),
                [_dot(w_lo, w2_ref[j, :, cols]) for j in range(2)],
                [_dot(a_lo, a2_ref[j, :, cols]) for j in range(2)])

    def tail(grp, k, w_lora, a_lora):
        cols = cols_of(grp)
        kkr = k * kk_ref[:, cols]
        sq = kkr * kkr
        hi = sq.astype(BF16)
        lo = (sq - hi.astype(F32)).astype(BF16)
        ss = _dot(hi, ones_bd) + _dot(lo, ones_bd)
        kkn = kkr * lax.rsqrt(jnp.maximum(ss, 1e-24))
        kkn_o[:, cols] = kkn.astype(BF16)
        k_b = k * ka_ref[:, cols]
        k_a = k - k_b
        for j, (kd_o, b_o, lw_o) in enumerate(((kd0_o, b0_o, lw0_o), (kd1_o, b1_o, lw1_o))):
            lw_o[:, cols] = -DECAY_SCALE * _sigmoid(w0_ref[j:j + 1, cols] + w_lora[j])
            a = _sigmoid(a0_ref[j:j + 1, cols] + a_lora[j])
            kd_o[:, cols] = (k_a + k_b * a).astype(BF16)
            b_o[:, cols] = (kkn * a).astype(BF16)

    pending = dots(0)
    for grp in range(N_GROUPS):
        nxt = dots(grp + 1) if grp + 1 < N_GROUPS else None
        tail(grp, *pending)
        pending = nxt


def _wkv_consts(reverse):
    row = lax.broadcasted_iota(jnp.int32, (CHUNK, GROUP_W), 0)
    s = lax.broadcasted_iota(jnp.int32, (CHUNK, GROUP_W), 1) % HEAD
    ri = lax.broadcasted_iota(jnp.int32, (CHUNK, CHUNK), 0)
    ci = lax.broadcasted_iota(jnp.int32, (CHUNK, CHUNK), 1)
    if reverse:
        strict, incl, tri = s > row, s >= row, ci >= ri
    else:
        strict, incl, tri = s < row, s <= row, ci <= ri
    return dict(strict=strict, incl=incl, eye=s == row, tri=jnp.where(tri, 1.0, 0.0).astype(BF16))


def _wkv_kernel(rf, vf, kkf, kdf, bf, lwf, rb, vb, kkb, kdb, bb, lwb, yf_o, yb_o, state, *, cps, wave):
    @pl.when(pl.program_id(1) == 0)
    def _():
        state[...] = jnp.zeros_like(state)

    consts = (_wkv_consts(False), _wkv_consts(True))
    dirs = ((rf, vf, kkf, kdf, bf, lwf, yf_o), (rb, vb, kkb, kdb, bb, lwb, yb_o))
    inst = [(d, g) for d in range(2) for g in range(N_GROUPS)]
    n = range(len(inst))
    half_w = GROUP_W // 2
    low_half = lax.broadcasted_iota(jnp.int32, (CHUNK, half_w), 1) < HEAD
    zero_half = jnp.zeros((CHUNK, half_w), BF16)

    def bd(y):
        yb = y.astype(BF16)
        parts = []
        for p in range(GROUP_HEADS):
            side = p // 2
            half = yb[:, side * half_w:(side + 1) * half_w]
            kept = jnp.where(low_half if p % 2 == 0 else ~low_half, half, zero_half)
            parts.append(jnp.concatenate([kept, zero_half] if side == 0 else [zero_half, kept], axis=1))
        return jnp.concatenate(parts, axis=0)

    def tr(y):
        yt = y.T
        return jnp.concatenate([yt[p * HEAD:(p + 1) * HEAD] for p in range(GROUP_HEADS)], axis=1)

    def mm(x, y):
        return _dot(x.astype(BF16), bd(y))

    def cat(*xs):
        return jnp.concatenate(xs, axis=0)

    def rows_of(d, k):
        j = k if d == 0 else cps - 1 - k
        return slice(j * CHUNK, (j + 1) * CHUNK)

    def cols_of(g):
        return slice(g * GROUP_W, (g + 1) * GROUP_W)

    def prologue(k, out):
        e = [dict(), dict()]
        for d in range(2):
            lw = dirs[d][5][rows_of(d, k), :]
            hi = lw.astype(BF16)
            lo = (lw - hi.astype(F32)).astype(BF16)
            tri = consts[d]["tri"]
            cum = _dot(tri, hi) + _dot(tri, lo)
            last = cum[0:1, :] if d == 1 else cum[CHUNK - 1:CHUNK, :]
            e[d].update(lw=lw, cum=cum, last=last)
        yield
        for d in range(2):
            e[d].update(e_pos=jnp.exp(e[d]["cum"]), e_neg=jnp.exp(-e[d]["cum"]))
        yield
        for d in range(2):
            e[d].update(e_excl=jnp.exp(e[d]["cum"] - e[d]["lw"]), e_last=jnp.exp(e[d]["last"] - e[d]["cum"]),
                        w_last=jnp.exp(e[d]["last"]))
        yield

        def load(idx, d, g):
            return dirs[d][idx][rows_of(d, k), cols_of(g)].astype(F32)

        def ex(name, d, g):
            return e[d][name][:, cols_of(g)]

        out["rd"] = [load(0, d, g) * ex("e_pos", d, g) for d, g in inst]
        out["kkd"] = [load(2, d, g) * ex("e_excl", d, g) for d, g in inst]
        out["v"] = [load(1, d, g) for d, g in inst]
        out["w_last"] = [ex("w_last", d, g) for d, g in inst]
        yield
        kd = [load(3, d, g) for d, g in inst]
        out["kdiv_t"] = [tr(kd[i] * ex("e_neg", *inst[i])) for i in n]
        out["kw_t"] = [tr(kd[i] * ex("e_last", *inst[i])) for i in n]
        yield
        b = [load(4, d, g) for d, g in inst]
        out["bdiv_t"] = [tr(b[i] * ex("e_neg", *inst[i])) for i in n]
        out["bw_t"] = [tr(b[i] * ex("e_last", *inst[i])) for i in n]
        yield

    def stages(wave, st):
        items = [(k, i) for k in wave for i in n]
        m = range(len(items))
        c = [consts[inst[i][0]] for _, i in items]

        def pr(name):
            return [pro[k][name][i] for k, i in items]

        rd, kkd, v = pr("rd"), pr("kkd"), pr("v")
        kdiv_t, bdiv_t, kw_t, bw_t, w_last = pr("kdiv_t"), pr("bdiv_t"), pr("kw_t"), pr("bw_t"), pr("w_last")
        x = [cat(kkd[j], rd[j]).astype(BF16) for j in m]
        sk = [mm(x[j], kdiv_t[j]) for j in m]
        yield
        sb = [mm(x[j], bdiv_t[j]) for j in m]
        yield
        m_kk = [jnp.where(c[j]["strict"], sk[j][:CHUNK], 0.0) for j in m]
        a_rk = [jnp.where(c[j]["incl"], sk[j][CHUNK:], 0.0) for j in m]
        a_rb = [jnp.where(c[j]["incl"], sb[j][CHUNK:], 0.0) for j in m]
        a = [jnp.where(c[j]["strict"], -sb[j][:CHUNK], 0.0) for j in m]
        tinv = [jnp.where(c[j]["eye"], 1.0, 0.0) + a[j] for j in m]
        ap = [mm(a[j], a[j]) for j in m]
        yield
        for step in range(5):
            if step < 4:
                both = [mm(cat(tinv[j], ap[j]), ap[j]) for j in m]
                tinv = [tinv[j] + both[j][:CHUNK] for j in m]
                ap = [both[j][CHUNK:] for j in m]
            else:
                tinv = [tinv[j] + mm(tinv[j], ap[j]) for j in m]
            yield
        xv = [mm(cat(m_kk[j], a_rk[j], kw_t[j]), v[j]) for j in m]
        yield
        uz = [mm(cat(a_rb[j], bw_t[j]), tinv[j]) for j in m]
        yield
        uzp = [mm(uz[j], kkd[j]) for j in m]
        uzq = [mm(uz[j], xv[j][:CHUNK]) for j in m]
        yield
        rp = [rd[j] - uzp[j][:CHUNK] for j in m]
        g_ = [jnp.where(c[j]["eye"], w_last[j], 0.0) - uzp[j][CHUNK:] for j in m]
        y0 = [xv[j][CHUNK:2 * CHUNK] - uzq[j][:CHUNK] for j in m]
        h_ = [xv[j][2 * CHUNK:] - uzq[j][CHUNK:] for j in m]
        for j, (k, i) in enumerate(items):
            d, g = inst[i]
            both = mm(cat(g_[j], rp[j]), st[i])
            st[i] = both[:CHUNK] + h_[j]
            dirs[d][6][rows_of(d, k), cols_of(g)] = (both[CHUNK:] + y0[j]).astype(BF16)
        yield

    def run_all(gens):
        for gen in gens:
            for _ in gen:
                pass

    pro = [dict() for _ in range(cps)]
    waves = [list(range(w, min(w + wave, cps))) for w in range(0, cps, wave)]
    run_all([prologue(k, pro[k]) for k in waves[0]])
    st = [state[d, g] for d, g in inst]
    for wi, wv in enumerate(waves):
        side = [prologue(k, pro[k]) for k in waves[wi + 1]] if wi + 1 < len(waves) else []
        for _ in stages(wv, st):
            for gen in side:
                next(gen, None)
        run_all(side)
    for i, (d, g) in enumerate(inst):
        state[d, g] = st[i]


def _rwkv_post_kernel(yf_ref, yb_ref, r_ref, v_ref, kd0_ref, kd1_ref, gate_ref, x_ref,
                      lnw_ref, lnb_ref, rk_ref, wo_ref, ones_ref, o_ref):
    ones_bd = ones_ref[...]
    y = yf_ref[...].astype(F32) + yb_ref[...].astype(F32)
    mean = _head_sums(y, ones_bd) * (1.0 / HEAD)
    dlt = y - mean
    var = _head_sums(dlt * dlt, ones_bd) * (1.0 / HEAD)
    yn = dlt * lax.rsqrt(var + GN_EPS) * lnw_ref[...] + lnb_ref[...]
    r = r_ref[...].astype(F32)
    ksum = kd0_ref[...].astype(F32) + kd1_ref[...].astype(F32)
    bonus = _head_sums(r * ksum * rk_ref[...], ones_bd) * v_ref[...].astype(F32)
    out = ((yn + bonus) * gate_ref[...].astype(F32)).astype(BF16)
    o_ref[...] = x_ref[...] + _dot(out, wo_ref[...])


def _rwkv_mix(x2d, g, mu, w_rkv, w_o, w0, w1, w2, a0, a1, a2, g1, g2, k_k, k_a, r_k, ln_w, ln_b,
              batch, seq):
    t, d = x2d.shape
    tm = 256
    tiles_per_seq = seq // tm
    lora = w1.shape[-1]
    n_dirs = w1.shape[0]
    row = lambda vec: vec.reshape(1, d)

    w1cat = jnp.concatenate([w1[j] for j in range(n_dirs)], axis=1).astype(BF16)
    a1cat = jnp.concatenate([a1[j] for j in range(n_dirs)], axis=1).astype(BF16)
    w2pad = jnp.stack([jnp.pad(w2[j], ((j * lora, (n_dirs - 1 - j) * lora), (0, 0))) for j in range(n_dirs)]).astype(BF16)
    a2pad = jnp.stack([jnp.pad(a2[j], ((j * lora, (n_dirs - 1 - j) * lora), (0, 0))) for j in range(n_dirs)]).astype(BF16)
    li = jnp.arange(GROUP_W, dtype=jnp.int32) // HEAD
    ones_bd = (li[:, None] == li[None, :]).astype(BF16)

    full = lambda a: pl.BlockSpec(a.shape, lambda i: (0,) * a.ndim)
    tile = pl.BlockSpec((tm, d), lambda i: (i, 0))
    n8 = t // 8
    prev8 = pl.BlockSpec((8, d), lambda i: (jnp.maximum(i * (tm // 8) - 1, 0), 0))
    next8 = pl.BlockSpec((8, d), lambda i: (jnp.minimum((i + 1) * (tm // 8), n8 - 1), 0))
    ti = jnp.arange(tm, dtype=jnp.int32)
    sbar = jnp.where(jnp.abs(ti[:, None] - ti[None, :]) == 1, 0.5, 0.0).astype(BF16)
    consts = (row(g), mu, sbar, w_rkv.astype(BF16), w1cat, w2pad, a1cat, a2pad, g1.astype(BF16), g2.astype(BF16),
              w0, a0, row(k_k), row(k_a), ones_bd)
    bf_out = jax.ShapeDtypeStruct((t, d), BF16)
    f32_out = jax.ShapeDtypeStruct((t, d), F32)
    r, v, kkn, gate, kd0, kd1, b0, b1, lw0, lw1 = pl.pallas_call(
        functools.partial(_rwkv_prep_kernel, tiles_per_seq=tiles_per_seq),
        grid=(t // tm,),
        in_specs=[tile, prev8, next8] + [full(a) for a in consts],
        out_specs=[tile] * 10,
        out_shape=[bf_out] * 8 + [f32_out] * 2,
        compiler_params=_params(),
        name="rwkv_prep",
    )(x2d, x2d, x2d, *consts)

    cps, wave = 4, 2
    nc = seq // (CHUNK * cps)
    fwd = pl.BlockSpec((CHUNK * cps, d), lambda b, c: (b * nc + c, 0))
    bwd = pl.BlockSpec((CHUNK * cps, d), lambda b, c: (b * nc + nc - 1 - c, 0))
    yf, yb = pl.pallas_call(
        functools.partial(_wkv_kernel, cps=cps, wave=wave),
        grid=(batch, nc),
        in_specs=[fwd] * 6 + [bwd] * 6,
        out_specs=[fwd, bwd],
        out_shape=[bf_out, bf_out],
        scratch_shapes=[pltpu.VMEM((2, N_GROUPS, HEAD, GROUP_W), F32)],
        compiler_params=_params(),
        name="wkv7_chunked",
    )(r, v, kkn, kd0, b0, lw0, r, v, kkn, kd1, b1, lw1)

    consts = (row(ln_w), row(ln_b), row(r_k), w_o.astype(BF16), ones_bd)
    return pl.pallas_call(
        _rwkv_post_kernel,
        grid=(t // tm,),
        in_specs=[tile] * 8 + [full(a) for a in consts],
        out_specs=tile,
        out_shape=f32_out,
        compiler_params=_params(),
        name="rwkv_post",
    )(yf, yb, r, v, kd0, kd1, gate, x2d, *consts)


def kernel(x, norm_mix_g, norm_ffn_g, norm_final_g, fno_w_out, rwkv_mu, rwkv_w_rkv, rwkv_w_o, rwkv_w0, rwkv_w1, rwkv_w2, rwkv_a0, rwkv_a1, rwkv_a2, rwkv_g1, rwkv_g2, rwkv_k_k, rwkv_k_a, rwkv_r_k, rwkv_ln_w, rwkv_ln_b, ffn_w_gate, ffn_w_up, ffn_w_down):
    batch, seq, d = x.shape
    depth = norm_mix_g.shape[0]
    h = x.reshape(batch * seq, d)
    for i in range(depth):
        j = i // 2
        if i % 2 == 0:
            h = _fourier_mix(h, norm_mix_g[i], fno_w_out[j], batch, seq)
        else:
            h = _rwkv_mix(h, norm_mix_g[i], rwkv_mu[j], rwkv_w_rkv[j], rwkv_w_o[j], rwkv_w0[j], rwkv_w1[j],
                          rwkv_w2[j], rwkv_a0[j], rwkv_a1[j], rwkv_a2[j], rwkv_g1[j], rwkv_g2[j],
                          rwkv_k_k[j], rwkv_k_a[j], rwkv_r_k[j], rwkv_ln_w[j], rwkv_ln_b[j], batch, seq)
        h = _ffn(h, norm_ffn_g[i], ffn_w_gate[i], ffn_w_up[i], ffn_w_down[i], norm_final_g,
                 final_norm=(i == depth - 1))
    return h.reshape(batch, seq, d)
```

```python
import functools
import math

import jax
import jax.numpy as jnp
from jax import lax
from jax.experimental import pallas as pl
from jax.experimental.pallas import tpu as pltpu

F32 = jnp.float32
BF16 = jnp.bfloat16

D_MODEL = 1024
HEAD = 64
GROUP_HEADS = 4
GROUP_W = GROUP_HEADS * HEAD
N_GROUPS = D_MODEL // GROUP_W
CHUNK = 64
FOURIER_GROUPS = 4
FOURIER_GW = D_MODEL // FOURIER_GROUPS
FOURIER_TILE = 512
BF16_ROWS = 16
RMS_EPS = 1e-6
GN_EPS = HEAD * 1e-5
DECAY_SCALE = math.exp(-0.5)
VMEM_LIMIT = 58 * 1024 * 1024


def _params():
    return pltpu.CompilerParams(vmem_limit_bytes=VMEM_LIMIT)


def _dot(a, b):
    return jnp.dot(a, b, preferred_element_type=F32)


def _rms(x, g):
    return x * lax.rsqrt(jnp.mean(x * x, axis=-1, keepdims=True) + RMS_EPS) * g


def _sigmoid(z):
    return 1.0 / (1.0 + jnp.exp(-z))


def _group_head_sums(ug, ones_bd):
    hi = ug.astype(BF16)
    lo = (ug - hi.astype(F32)).astype(BF16)
    return _dot(hi, ones_bd) + _dot(lo, ones_bd)


def _head_sums(u, ones_bd):
    return jnp.concatenate([_group_head_sums(u[:, g * GROUP_W:(g + 1) * GROUP_W], ones_bd)
                            for g in range(N_GROUPS)], axis=1)


def _fourier_kernel(x_ref, g_ref, ccsc_ref, ch_ref, sh_ref, rev_ref, w_ref, o_ref, zc, zs):
    i = pl.program_id(1)
    ft = FOURIER_TILE
    seq = x_ref.shape[1]
    n_tiles = seq // ft

    @pl.when(i == 0)
    def _():
        for t in range(n_tiles):
            rows = slice(t * ft, (t + 1) * ft)
            h = _rms(x_ref[0, rows, :], g_ref[...]).astype(BF16)
            for g in range(FOURIER_GROUPS):
                cols = slice(g * FOURIER_GW, (g + 1) * FOURIER_GW)
                z = _dot(h[:, cols], ccsc_ref[...])
                zc[rows, cols] = z[:, :FOURIER_GW].astype(BF16)
                zs[rows, cols] = z[:, FOURIER_GW:].astype(BF16)

    t1 = _dot(ch_ref[0], zc[...])
    t2 = _dot(sh_ref[0], zs[...])
    w = w_ref[...]
    top = (t1[:ft] - t2[:ft]).astype(BF16)
    r_top = pl.multiple_of(i * ft, ft)
    o_ref[0, 0, 0] = x_ref[0, pl.ds(r_top, ft), :] + _dot(top, w)
    bot = _dot(rev_ref[...], (t1 + t2).astype(BF16)).astype(BF16)
    r_bot = pl.multiple_of((n_tiles - 1 - i) * ft, ft)
    o_ref[0, 1, 0] = x_ref[0, pl.ds(r_bot, ft), :] + _dot(bot, w)


def _fourier_mix(x2d, g, w_out, batch, seq):
    t, d = x2d.shape
    ft = FOURIER_TILE
    ft_pad = ft + BF16_ROWS
    nh = seq // (2 * ft)
    n = jnp.arange(FOURIER_GW, dtype=jnp.int32)
    ang = ((n[:, None] * n[None, :]) % FOURIER_GW).astype(F32) * (2.0 * math.pi / FOURIER_GW)
    ccsc = (jnp.concatenate([jnp.cos(ang), jnp.sin(ang)], axis=1) * (FOURIER_GW ** -0.5)).astype(BF16)
    k = (jnp.arange(nh, dtype=jnp.int32)[:, None] * ft + jnp.arange(ft_pad, dtype=jnp.int32)[None, :]) % seq
    n = jnp.arange(seq, dtype=jnp.int32)
    ang = ((k[:, :, None] * n[None, None, :]) % seq).astype(F32) * (2.0 * math.pi / seq)
    ch = (jnp.cos(ang) * (seq ** -0.5)).astype(BF16)
    sh = (jnp.sin(ang) * (seq ** -0.5)).astype(BF16)
    rev = (jnp.arange(ft, dtype=jnp.int32)[:, None] + jnp.arange(ft_pad, dtype=jnp.int32)[None, :] == ft).astype(BF16)

    const2 = lambda a: pl.BlockSpec(a.shape, lambda b, i: (0,) * a.ndim)
    g2d = g.reshape(1, d)
    w_bf = w_out.astype(BF16)
    out = pl.pallas_call(
        _fourier_kernel,
        grid=(batch, nh),
        in_specs=[pl.BlockSpec((1, seq, d), lambda b, i: (b, 0, 0)),
                  const2(g2d), const2(ccsc),
                  pl.BlockSpec((1, ft_pad, seq), lambda b, i: (i, 0, 0)),
                  pl.BlockSpec((1, ft_pad, seq), lambda b, i: (i, 0, 0)),
                  const2(rev), const2(w_bf)],
        out_specs=pl.BlockSpec((1, 2, 1, ft, d), lambda b, i: (b, 0, i, 0, 0)),
        out_shape=jax.ShapeDtypeStruct((batch, 2, nh, ft, d), F32),
        scratch_shapes=[pltpu.VMEM((seq, d), BF16), pltpu.VMEM((seq, d), BF16)],
        compiler_params=_params(),
        name="fourier_mix",
    )(x2d.reshape(batch, seq, d), g2d, ccsc, ch, sh, rev, w_bf)
    return out.reshape(t, d)


def _mirror_tile_map(tm, seq):
    per = FOURIER_TILE // tm
    n_tiles = seq // FOURIER_TILE
    nh = n_tiles // 2
    tiles_per_seq = seq // tm

    def index_map(i):
        b, u = i // tiles_per_seq, i % tiles_per_seq
        j, sub = u // per, u % per
        stored = jnp.where(j < nh, j, nh + n_tiles - 1 - j)
        return (b * tiles_per_seq + stored * per + sub, 0)

    return index_map


def _ffn_kernel(x_ref, g_ref, wg_ref, wu_ref, wd_ref, gf_ref, o_ref, *, final_norm):
    x = x_ref[...]
    h = _rms(x, g_ref[...]).astype(BF16)
    gate = _dot(h, wg_ref[...])
    up = _dot(h, wu_ref[...])
    act = (gate * _sigmoid(gate) * up).astype(BF16)
    y = x + _dot(act, wd_ref[...])
    if final_norm:
        y = _rms(y, gf_ref[...])
    o_ref[...] = y


def _ffn(x2d, g, w_gate, w_up, w_down, g_final, final_norm, x_index_map=None):
    t, d = x2d.shape
    f = w_gate.shape[1]
    tm = 256
    if x_index_map is None:
        x_index_map = lambda i: (i, 0)
    else:
        x_index_map = x_index_map(tm)
    return pl.pallas_call(
        functools.partial(_ffn_kernel, final_norm=final_norm),
        grid=(t // tm,),
        in_specs=[pl.BlockSpec((tm, d), x_index_map),
                  pl.BlockSpec((1, d), lambda i: (0, 0)),
                  pl.BlockSpec((d, f), lambda i: (0, 0)),
                  pl.BlockSpec((d, f), lambda i: (0, 0)),
                  pl.BlockSpec((f, d), lambda i: (0, 0)),
                  pl.BlockSpec((1, d), lambda i: (0, 0))],
        out_specs=pl.BlockSpec((tm, d), lambda i: (i, 0)),
        out_shape=jax.ShapeDtypeStruct((t, d), F32),
        compiler_params=_params(),
        name="swiglu_ffn",
    )(x2d, g.reshape(1, d), w_gate.astype(BF16), w_up.astype(BF16), w_down.astype(BF16),
      g_final.reshape(1, d))


def _rwkv_prep_kernel(x_ref, xp_ref, xn_ref, g_ref, mu_ref, sbar_ref, wrkv_ref, w1_ref, w2_ref, a1_ref, a2_ref,
                      g1_ref, g2_ref, w0_ref, a0_ref, kk_ref, ka_ref, ones_ref,
                      rvk_o, gate_o, kb0_o, kb1_o, lw0_o, lw1_o, *, tiles_per_seq):
    i = pl.program_id(0)
    tm, d = x_ref.shape
    g = g_ref[...]
    h = _rms(x_ref[...], g)
    hb = h.astype(BF16)
    pos = i % tiles_per_seq
    keep_prev = jnp.where(pos == 0, 0.0, 0.5)
    keep_next = jnp.where(pos == tiles_per_seq - 1, 0.0, 0.5)
    h_prev = _rms(xp_ref[...], g)[7:8, :] * keep_prev
    h_next = _rms(xn_ref[...], g)[0:1, :] * keep_next
    hbar = _dot(sbar_ref[...], hb)
    row8 = lax.broadcasted_iota(jnp.int32, (8, d), 0)
    hbar = jnp.concatenate([hbar[:8] + jnp.where(row8 == 0, h_prev, 0.0), hbar[8:tm - 8],
                            hbar[tm - 8:] + jnp.where(row8 == 7, h_next, 0.0)], axis=0)
    xx = (hbar - h).astype(BF16)
    mu = mu_ref[...].astype(BF16)

    def mix(c):
        return hb + xx * mu[c:c + 1, :]

    xr, xk, xv = mix(0), mix(1), mix(2)
    w_lo = jnp.tanh(_dot(mix(3), w1_ref[...])).astype(BF16)
    a_lo = _dot(mix(4), a1_ref[...]).astype(BF16)
    g_lo = _sigmoid(_dot(mix(5), g1_ref[...])).astype(BF16)
    ones_bd = ones_ref[...]

    def cols_of(grp, part=0):
        return slice(part * d + grp * GROUP_W, part * d + (grp + 1) * GROUP_W)

    def dots(grp):
        cols = cols_of(grp)
        rvk_o[:, cols] = _dot(xr, wrkv_ref[0, :, cols]).astype(BF16)
        rvk_o[:, cols_of(grp, 1)] = _dot(xv, wrkv_ref[2, :, cols]).astype(BF16)
        gate_o[:, cols] = _dot(g_lo, g2_ref[:, cols]).astype(BF16)
        return (_dot(xk, wrkv_ref[1, :, cols]),
                [_dot(w_lo, w2_ref[j, :, cols]) for j in range(2)],
                [_dot(a_lo, a2_ref[j, :, cols]) for j in range(2)])

    def tail(grp, k, w_lora, a_lora):
        cols = cols_of(grp)
        kkr = k * kk_ref[:, cols]
        kkn = kkr * lax.rsqrt(jnp.maximum(_group_head_sums(kkr * kkr, ones_bd), 1e-24))
        rvk_o[:, cols_of(grp, 2)] = kkn.astype(BF16)
        k_b = k * ka_ref[:, cols]
        k_a = k - k_b
        for j, (kb_o, lw_o) in enumerate(((kb0_o, lw0_o), (kb1_o, lw1_o))):
            lw_o[:, cols] = -DECAY_SCALE * _sigmoid(w0_ref[j:j + 1, cols] + w_lora[j])
            a = _sigmoid(a0_ref[j:j + 1, cols] + a_lora[j])
            kb_o[:, cols] = (k_a + k_b * a).astype(BF16)
            kb_o[:, cols_of(grp, 1)] = (kkn * a).astype(BF16)

    pending = dots(0)
    for grp in range(N_GROUPS):
        nxt = dots(grp + 1) if grp + 1 < N_GROUPS else None
        tail(grp, *pending)
        pending = nxt


def _wkv_consts(reverse):
    row = lax.broadcasted_iota(jnp.int32, (CHUNK, GROUP_W), 0)
    s = lax.broadcasted_iota(jnp.int32, (CHUNK, GROUP_W), 1) % HEAD
    ri = lax.broadcasted_iota(jnp.int32, (CHUNK, CHUNK), 0)
    ci = lax.broadcasted_iota(jnp.int32, (CHUNK, CHUNK), 1)
    if reverse:
        strict, incl, tri = s > row, s >= row, ci >= ri
    else:
        strict, incl, tri = s < row, s <= row, ci <= ri
    return dict(strict=strict, incl=incl, eye=s == row, tri=jnp.where(tri, 1.0, 0.0).astype(BF16))


def _wkv_kernel(rvkf, kbf, lwf, rvkb, kbb, lwb, yf_o, yb_o, state, *, cps, wave):
    @pl.when(pl.program_id(1) == 0)
    def _():
        state[...] = jnp.zeros_like(state)

    d_model = lwf.shape[1]
    consts = (_wkv_consts(False), _wkv_consts(True))
    dirs = ((rvkf, kbf, lwf, yf_o), (rvkb, kbb, lwb, yb_o))
    inst = [(d, g) for d in range(2) for g in range(N_GROUPS)]
    n = range(len(inst))
    half_w = GROUP_W // 2
    low_half = lax.broadcasted_iota(jnp.int32, (CHUNK, half_w), 1) < HEAD
    zero_half = jnp.zeros((CHUNK, half_w), BF16)

    def bd(y):
        yb = y.astype(BF16)
        parts = []
        for p in range(GROUP_HEADS):
            side = p // 2
            half = yb[:, side * half_w:(side + 1) * half_w]
            kept = jnp.where(low_half if p % 2 == 0 else ~low_half, half, zero_half)
            parts.append(jnp.concatenate([kept, zero_half] if side == 0 else [zero_half, kept], axis=1))
        return jnp.concatenate(parts, axis=0)

    def tr(y):
        yt = y.T
        return jnp.concatenate([yt[p * HEAD:(p + 1) * HEAD] for p in range(GROUP_HEADS)], axis=1)

    def mm(x, y):
        return _dot(x, bd(y))

    def cat(*xs):
        return jnp.concatenate(xs, axis=0)

    def bf(y):
        return y.astype(BF16)

    def rows_of(d, k):
        j = k if d == 0 else cps - 1 - k
        return slice(j * CHUNK, (j + 1) * CHUNK)

    def cols_of(g, part=0):
        return slice(part * d_model + g * GROUP_W, part * d_model + (g + 1) * GROUP_W)

    def prologue(k, out):
        e = [dict(), dict()]
        for d in range(2):
            lw = dirs[d][2][rows_of(d, k), :]
            hi = lw.astype(BF16)
            lo = (lw - hi.astype(F32)).astype(BF16)
            tri = consts[d]["tri"]
            cum = _dot(tri, hi) + _dot(tri, lo)
            last = cum[0:1, :] if d == 1 else cum[CHUNK - 1:CHUNK, :]
            e[d].update(lw=lw, cum=cum, last=last)
        yield
        for d in range(2):
            e[d].update(e_pos=jnp.exp(e[d]["cum"]), e_neg=jnp.exp(-e[d]["cum"]))
        yield
        for d in range(2):
            e[d].update(e_excl=jnp.exp(e[d]["cum"] - e[d]["lw"]), e_last=jnp.exp(e[d]["last"] - e[d]["cum"]),
                        w_last=jnp.exp(e[d]["last"]))
        yield

        def load(ref_idx, part, d, g):
            return dirs[d][ref_idx][rows_of(d, k), cols_of(g, part)].astype(F32)

        def ex(name, d, g):
            return e[d][name][:, cols_of(g)]

        out["rd"] = [load(0, 0, d, g) * ex("e_pos", d, g) for d, g in inst]
        out["kkd"] = [bf(load(0, 2, d, g) * ex("e_excl", d, g)) for d, g in inst]
        out["v"] = [dirs[d][0][rows_of(d, k), cols_of(g, 1)] for d, g in inst]
        out["w_last"] = [ex("w_last", d, g) for d, g in inst]
        yield
        kd = [load(1, 0, d, g) for d, g in inst]
        out["kdiv_t"] = [bf(tr(kd[i] * ex("e_neg", *inst[i]))) for i in n]
        out["kw_t"] = [bf(tr(kd[i] * ex("e_last", *inst[i]))) for i in n]
        yield
        b = [load(1, 1, d, g) for d, g in inst]
        out["bdiv_t"] = [bf(tr(b[i] * ex("e_neg", *inst[i]))) for i in n]
        out["bw_t"] = [bf(tr(b[i] * ex("e_last", *inst[i]))) for i in n]
        yield

    def stages(wave_chunks, st):
        items = [(k, i) for k in wave_chunks for i in n]
        m = range(len(items))
        c = [consts[inst[i][0]] for _, i in items]

        def pr(name):
            return [pro[k][name][i] for k, i in items]

        rd, kkd, v = pr("rd"), pr("kkd"), pr("v")
        kdiv_t, bdiv_t, kw_t, bw_t, w_last = pr("kdiv_t"), pr("bdiv_t"), pr("kw_t"), pr("bw_t"), pr("w_last")
        x = [cat(kkd[j], bf(rd[j])) for j in m]
        sk = [mm(x[j], kdiv_t[j]) for j in m]
        yield
        sb = [mm(x[j], bdiv_t[j]) for j in m]
        yield
        m_kk = [bf(jnp.where(c[j]["strict"], sk[j][:CHUNK], 0.0)) for j in m]
        a_rk = [bf(jnp.where(c[j]["incl"], sk[j][CHUNK:], 0.0)) for j in m]
        a_rb = [bf(jnp.where(c[j]["incl"], sb[j][CHUNK:], 0.0)) for j in m]
        a = [jnp.where(c[j]["strict"], -sb[j][:CHUNK], 0.0) for j in m]
        tinv = [jnp.where(c[j]["eye"], 1.0, 0.0) + a[j] for j in m]
        ap = [bf(a[j]) for j in m]
        ap = [bf(mm(ap[j], ap[j])) for j in m]
        yield
        for step in range(5):
            if step < 4:
                both = [mm(cat(bf(tinv[j]), ap[j]), ap[j]) for j in m]
                tinv = [tinv[j] + both[j][:CHUNK] for j in m]
                ap = [bf(both[j][CHUNK:]) for j in m]
            else:
                tinv = [tinv[j] + mm(bf(tinv[j]), ap[j]) for j in m]
            yield
        xv = [mm(cat(m_kk[j], a_rk[j], kw_t[j]), v[j]) for j in m]
        yield
        uz = [bf(mm(cat(a_rb[j], bw_t[j]), tinv[j])) for j in m]
        yield
        uzp = [mm(uz[j], kkd[j]) for j in m]
        uzq = [mm(uz[j], xv[j][:CHUNK]) for j in m]
        yield
        rp = [rd[j] - uzp[j][:CHUNK] for j in m]
        g_ = [jnp.where(c[j]["eye"], w_last[j], 0.0) - uzp[j][CHUNK:] for j in m]
        y0 = [xv[j][CHUNK:2 * CHUNK] - uzq[j][:CHUNK] for j in m]
        h_ = [xv[j][2 * CHUNK:] - uzq[j][CHUNK:] for j in m]
        for j, (k, i) in enumerate(items):
            d, g = inst[i]
            both = mm(bf(cat(g_[j], rp[j])), st[i])
            st[i] = both[:CHUNK] + h_[j]
            dirs[d][3][rows_of(d, k), cols_of(g)] = (both[CHUNK:] + y0[j]).astype(BF16)
        yield

    def run_all(gens):
        for gen in gens:
            for _ in gen:
                pass

    pro = [dict() for _ in range(cps)]
    waves = [list(range(w, min(w + wave, cps))) for w in range(0, cps, wave)]
    run_all([prologue(k, pro[k]) for k in waves[0]])
    st = [state[d, g] for d, g in inst]
    for wi, wv in enumerate(waves):
        side = [prologue(k, pro[k]) for k in waves[wi + 1]] if wi + 1 < len(waves) else []
        for _ in stages(wv, st):
            for gen in side:
                next(gen, None)
        run_all(side)
    for i, (d, g) in enumerate(inst):
        state[d, g] = st[i]


def _rwkv_post_kernel(yf_ref, yb_ref, r_ref, v_ref, kd0_ref, kd1_ref, gate_ref, x_ref,
                      lnw_ref, lnb_ref, rk_ref, wo_ref, ones_ref, o_ref):
    ones_bd = ones_ref[...]
    y = yf_ref[...].astype(F32) + yb_ref[...].astype(F32)
    mean = _head_sums(y, ones_bd) * (1.0 / HEAD)
    dlt = y - mean
    var = _head_sums(dlt * dlt, ones_bd) * (1.0 / HEAD)
    yn = dlt * lax.rsqrt(var + GN_EPS) * lnw_ref[...] + lnb_ref[...]
    r = r_ref[...].astype(F32)
    ksum = kd0_ref[...].astype(F32) + kd1_ref[...].astype(F32)
    bonus = _head_sums(r * ksum * rk_ref[...], ones_bd) * v_ref[...].astype(F32)
    out = ((yn + bonus) * gate_ref[...].astype(F32)).astype(BF16)
    o_ref[...] = x_ref[...] + _dot(out, wo_ref[...])


def _rwkv_mix(x2d, g, mu, w_rkv, w_o, w0, w1, w2, a0, a1, a2, g1, g2, k_k, k_a, r_k, ln_w, ln_b,
              batch, seq):
    t, d = x2d.shape
    tm = 256
    tiles_per_seq = seq // tm
    lora = w1.shape[-1]
    n_dirs = w1.shape[0]
    row = lambda vec: vec.reshape(1, d)

    w1cat = jnp.concatenate([w1[j] for j in range(n_dirs)], axis=1).astype(BF16)
    a1cat = jnp.concatenate([a1[j] for j in range(n_dirs)], axis=1).astype(BF16)
    w2pad = jnp.stack([jnp.pad(w2[j], ((j * lora, (n_dirs - 1 - j) * lora), (0, 0))) for j in range(n_dirs)]).astype(BF16)
    a2pad = jnp.stack([jnp.pad(a2[j], ((j * lora, (n_dirs - 1 - j) * lora), (0, 0))) for j in range(n_dirs)]).astype(BF16)
    li = jnp.arange(GROUP_W, dtype=jnp.int32) // HEAD
    ones_bd = (li[:, None] == li[None, :]).astype(BF16)

    full = lambda a: pl.BlockSpec(a.shape, lambda i: (0,) * a.ndim)
    tile = lambda part: pl.BlockSpec((tm, d), lambda i: (i, part))
    wide = lambda parts: pl.BlockSpec((tm, parts * d), lambda i: (i, 0))
    n8 = t // 8
    prev8 = pl.BlockSpec((8, d), lambda i: (jnp.maximum(i * (tm // 8) - 1, 0), 0))
    next8 = pl.BlockSpec((8, d), lambda i: (jnp.minimum((i + 1) * (tm // 8), n8 - 1), 0))
    ti = jnp.arange(tm, dtype=jnp.int32)
    sbar = jnp.where(jnp.abs(ti[:, None] - ti[None, :]) == 1, 0.5, 0.0).astype(BF16)
    consts = (row(g), mu, sbar, w_rkv.astype(BF16), w1cat, w2pad, a1cat, a2pad, g1.astype(BF16), g2.astype(BF16),
              w0, a0, row(k_k), row(k_a), ones_bd)
    bf_out = lambda parts: jax.ShapeDtypeStruct((t, parts * d), BF16)
    f32_out = jax.ShapeDtypeStruct((t, d), F32)
    rvk, gate, kb0, kb1, lw0, lw1 = pl.pallas_call(
        functools.partial(_rwkv_prep_kernel, tiles_per_seq=tiles_per_seq),
        grid=(t // tm,),
        in_specs=[tile(0), prev8, next8] + [full(a) for a in consts],
        out_specs=[wide(3), tile(0), wide(2), wide(2), tile(0), tile(0)],
        out_shape=[bf_out(3), bf_out(1), bf_out(2), bf_out(2), f32_out, f32_out],
        compiler_params=_params(),
        name="rwkv_prep",
    )(x2d, x2d, x2d, *consts)

    cps, wave = 4, 2
    nc = seq // (CHUNK * cps)
    fwd = lambda parts: pl.BlockSpec((CHUNK * cps, parts * d), lambda b, c: (b * nc + c, 0))
    bwd = lambda parts: pl.BlockSpec((CHUNK * cps, parts * d), lambda b, c: (b * nc + nc - 1 - c, 0))
    yf, yb = pl.pallas_call(
        functools.partial(_wkv_kernel, cps=cps, wave=wave),
        grid=(batch, nc),
        in_specs=[fwd(3), fwd(2), fwd(1), bwd(3), bwd(2), bwd(1)],
        out_specs=[fwd(1), bwd(1)],
        out_shape=[bf_out(1), bf_out(1)],
        scratch_shapes=[pltpu.VMEM((2, N_GROUPS, HEAD, GROUP_W), F32)],
        compiler_params=_params(),
        name="wkv7_chunked",
    )(rvk, kb0, lw0, rvk, kb1, lw1)

    consts = (row(ln_w), row(ln_b), row(r_k), w_o.astype(BF16), ones_bd)
    return pl.pallas_call(
        _rwkv_post_kernel,
        grid=(t // tm,),
        in_specs=[tile(0)] * 2 + [tile(0), tile(1), tile(0), tile(0), tile(0), tile(0)] + [full(a) for a in consts],
        out_specs=tile(0),
        out_shape=f32_out,
        compiler_params=_params(),
        name="rwkv_post",
    )(yf, yb, rvk, rvk, kb0, kb1, gate, x2d, *consts)


def kernel(x, norm_mix_g, norm_ffn_g, norm_final_g, fno_w_out, rwkv_mu, rwkv_w_rkv, rwkv_w_o, rwkv_w0, rwkv_w1, rwkv_w2, rwkv_a0, rwkv_a1, rwkv_a2, rwkv_g1, rwkv_g2, rwkv_k_k, rwkv_k_a, rwkv_r_k, rwkv_ln_w, rwkv_ln_b, ffn_w_gate, ffn_w_up, ffn_w_down):
    batch, seq, d = x.shape
    depth = norm_mix_g.shape[0]
    h = x.reshape(batch * seq, d)
    for i in range(depth):
        j = i // 2
        x_index_map = None
        if i % 2 == 0:
            h = _fourier_mix(h, norm_mix_g[i], fno_w_out[j], batch, seq)
            x_index_map = functools.partial(_mirror_tile_map, seq=seq)
        else:
            h = _rwkv_mix(h, norm_mix_g[i], rwkv_mu[j], rwkv_w_rkv[j], rwkv_w_o[j], rwkv_w0[j], rwkv_w1[j],
                          rwkv_w2[j], rwkv_a0[j], rwkv_a1[j], rwkv_a2[j], rwkv_g1[j], rwkv_g2[j],
                          rwkv_k_k[j], rwkv_k_a[j], rwkv_r_k[j], rwkv_ln_w[j], rwkv_ln_b[j], batch, seq)
        h = _ffn(h, norm_ffn_g[i], ffn_w_gate[i], ffn_w_up[i], ffn_w_down[i], norm_final_g,
                 final_norm=(i == depth - 1), x_index_map=x_index_map)
    return h.reshape(batch, seq, d)
```

```python
import functools
import math

import jax
import jax.numpy as jnp
from jax import lax
from jax.experimental import pallas as pl
from jax.experimental.pallas import tpu as pltpu

F32 = jnp.float32
BF16 = jnp.bfloat16

D_MODEL = 1024
HEAD = 64
GROUP_HEADS = 4
GROUP_W = GROUP_HEADS * HEAD
N_GROUPS = D_MODEL // GROUP_W
CHUNK = 64
ROW_TILE = 512
FOURIER_GROUPS = 4
FOURIER_GW = D_MODEL // FOURIER_GROUPS
FOURIER_TILE = 512
BF16_ROWS = 16
RMS_EPS = 1e-6
GN_EPS = HEAD * 1e-5
DECAY_SCALE = math.exp(-0.5)
VMEM_LIMIT = 58 * 1024 * 1024


def _params():
    return pltpu.CompilerParams(vmem_limit_bytes=VMEM_LIMIT)


def _dot(a, b):
    return jnp.dot(a, b, preferred_element_type=F32)


def _rms(x, g):
    return x * lax.rsqrt(jnp.mean(x * x, axis=-1, keepdims=True) + RMS_EPS) * g


def _sigmoid(z):
    return 1.0 / (1.0 + jnp.exp(-z))


def _group_head_sums(ug, ones_bd):
    hi = ug.astype(BF16)
    lo = (ug - hi.astype(F32)).astype(BF16)
    return _dot(hi, ones_bd) + _dot(lo, ones_bd)


def _head_sums(u, ones_bd):
    return jnp.concatenate([_group_head_sums(u[:, g * GROUP_W:(g + 1) * GROUP_W], ones_bd)
                            for g in range(N_GROUPS)], axis=1)


def _fourier_kernel(x_ref, g_ref, ccsc_ref, ch_ref, sh_ref, rev_ref, w_ref, o_ref, zc, zs):
    i = pl.program_id(1)
    ft = FOURIER_TILE
    seq = x_ref.shape[1]
    n_tiles = seq // ft

    @pl.when(i == 0)
    def _():
        for t in range(n_tiles):
            rows = slice(t * ft, (t + 1) * ft)
            h = _rms(x_ref[0, rows, :], g_ref[...]).astype(BF16)
            for g in range(FOURIER_GROUPS):
                cols = slice(g * FOURIER_GW, (g + 1) * FOURIER_GW)
                z = _dot(h[:, cols], ccsc_ref[...])
                zc[rows, cols] = z[:, :FOURIER_GW].astype(BF16)
                zs[rows, cols] = z[:, FOURIER_GW:].astype(BF16)

    t1 = _dot(ch_ref[0], zc[...])
    t2 = _dot(sh_ref[0], zs[...])
    w = w_ref[...]
    top = (t1[:ft] - t2[:ft]).astype(BF16)
    r_top = pl.multiple_of(i * ft, ft)
    o_ref[0, 0, 0] = x_ref[0, pl.ds(r_top, ft), :] + _dot(top, w)
    bot = _dot(rev_ref[...], (t1 + t2).astype(BF16)).astype(BF16)
    r_bot = pl.multiple_of((n_tiles - 1 - i) * ft, ft)
    o_ref[0, 1, 0] = x_ref[0, pl.ds(r_bot, ft), :] + _dot(bot, w)


def _fourier_mix(x2d, g, w_out, batch, seq):
    t, d = x2d.shape
    ft = FOURIER_TILE
    ft_pad = ft + BF16_ROWS
    nh = seq // (2 * ft)
    n = jnp.arange(FOURIER_GW, dtype=jnp.int32)
    ang = ((n[:, None] * n[None, :]) % FOURIER_GW).astype(F32) * (2.0 * math.pi / FOURIER_GW)
    ccsc = (jnp.concatenate([jnp.cos(ang), jnp.sin(ang)], axis=1) * (FOURIER_GW ** -0.5)).astype(BF16)
    n = jnp.arange(seq, dtype=jnp.int32)
    blk = 32
    n_hi = -(-((nh - 1) * ft + ft_pad) // blk)

    def table(kvec):
        ang = ((kvec[:, None] * n[None, :]) % seq).astype(F32) * (2.0 * math.pi / seq)
        return jnp.cos(ang), jnp.sin(ang)

    c_hi, s_hi = (tab[:, None, :] for tab in table(jnp.arange(n_hi, dtype=jnp.int32) * blk))
    c_lo, s_lo = (tab[None, :, :] for tab in table(jnp.arange(blk, dtype=jnp.int32)))
    cos_k = ((c_hi * c_lo - s_hi * s_lo) * (seq ** -0.5)).reshape(n_hi * blk, seq)
    sin_k = ((s_hi * c_lo + c_hi * s_lo) * (seq ** -0.5)).reshape(n_hi * blk, seq)
    ch = jnp.stack([cos_k[i * ft:i * ft + ft_pad] for i in range(nh)]).astype(BF16)
    sh = jnp.stack([sin_k[i * ft:i * ft + ft_pad] for i in range(nh)]).astype(BF16)
    rev = (jnp.arange(ft, dtype=jnp.int32)[:, None] + jnp.arange(ft_pad, dtype=jnp.int32)[None, :] == ft).astype(BF16)

    const2 = lambda a: pl.BlockSpec(a.shape, lambda b, i: (0,) * a.ndim)
    g2d = g.reshape(1, d)
    w_bf = w_out.astype(BF16)
    out = pl.pallas_call(
        _fourier_kernel,
        grid=(batch, nh),
        in_specs=[pl.BlockSpec((1, seq, d), lambda b, i: (b, 0, 0)),
                  const2(g2d), const2(ccsc),
                  pl.BlockSpec((1, ft_pad, seq), lambda b, i: (i, 0, 0)),
                  pl.BlockSpec((1, ft_pad, seq), lambda b, i: (i, 0, 0)),
                  const2(rev), const2(w_bf)],
        out_specs=pl.BlockSpec((1, 2, 1, ft, d), lambda b, i: (b, 0, i, 0, 0)),
        out_shape=jax.ShapeDtypeStruct((batch, 2, nh, ft, d), F32),
        scratch_shapes=[pltpu.VMEM((seq, d), BF16), pltpu.VMEM((seq, d), BF16)],
        compiler_params=_params(),
        name="fourier_mix",
    )(x2d.reshape(batch, seq, d), g2d, ccsc, ch, sh, rev, w_bf)
    return out.reshape(t, d)


def _mirror_tile_map(tm, seq):
    per = FOURIER_TILE // tm
    n_tiles = seq // FOURIER_TILE
    nh = n_tiles // 2
    tiles_per_seq = seq // tm

    def index_map(i):
        b, u = i // tiles_per_seq, i % tiles_per_seq
        j, sub = u // per, u % per
        stored = jnp.where(j < nh, j, nh + n_tiles - 1 - j)
        return (b * tiles_per_seq + stored * per + sub, 0)

    return index_map


def _ffn_kernel(x_ref, g_ref, wg_ref, wu_ref, wd_ref, gf_ref, o_ref, *, final_norm):
    x = x_ref[...]
    h = _rms(x, g_ref[...]).astype(BF16)
    gate = _dot(h, wg_ref[...])
    up = _dot(h, wu_ref[...])
    act = (gate * _sigmoid(gate) * up).astype(BF16)
    y = x + _dot(act, wd_ref[...])
    if final_norm:
        y = _rms(y, gf_ref[...])
    o_ref[...] = y


def _ffn(x2d, g, w_gate, w_up, w_down, g_final, final_norm, x_index_map=None):
    t, d = x2d.shape
    f = w_gate.shape[1]
    tm = ROW_TILE
    if x_index_map is None:
        x_index_map = lambda i: (i, 0)
    else:
        x_index_map = x_index_map(tm)
    resident = lambda shape: pl.BlockSpec(shape, lambda i: (0, 0), pipeline_mode=pl.Buffered(1))
    return pl.pallas_call(
        functools.partial(_ffn_kernel, final_norm=final_norm),
        grid=(t // tm,),
        in_specs=[pl.BlockSpec((tm, d), x_index_map),
                  pl.BlockSpec((1, d), lambda i: (0, 0)),
                  resident((d, f)),
                  resident((d, f)),
                  resident((f, d)),
                  pl.BlockSpec((1, d), lambda i: (0, 0))],
        out_specs=pl.BlockSpec((tm, d), lambda i: (i, 0)),
        out_shape=jax.ShapeDtypeStruct((t, d), F32),
        compiler_params=_params(),
        name="swiglu_ffn",
    )(x2d, g.reshape(1, d), w_gate.astype(BF16), w_up.astype(BF16), w_down.astype(BF16),
      g_final.reshape(1, d))


def _rwkv_prep_kernel(x_ref, xp_ref, xn_ref, g_ref, mu_ref, sbar_ref, wrkv_ref, w1_ref, w2_ref, a1_ref, a2_ref,
                      g1_ref, g2_ref, w0_ref, a0_ref, kk_ref, ka_ref, ones_ref,
                      rvk_o, gate_o, kb0_o, kb1_o, lw0_o, lw1_o, *, tiles_per_seq):
    i = pl.program_id(0)
    tm, d = x_ref.shape
    g = g_ref[...]
    h = _rms(x_ref[...], g)
    hb = h.astype(BF16)
    pos = i % tiles_per_seq
    keep_prev = jnp.where(pos == 0, 0.0, 0.5)
    keep_next = jnp.where(pos == tiles_per_seq - 1, 0.0, 0.5)
    h_prev = _rms(xp_ref[...], g)[7:8, :] * keep_prev
    h_next = _rms(xn_ref[...], g)[0:1, :] * keep_next
    hbar = _dot(sbar_ref[...], hb)
    row8 = lax.broadcasted_iota(jnp.int32, (8, d), 0)
    hbar = jnp.concatenate([hbar[:8] + jnp.where(row8 == 0, h_prev, 0.0), hbar[8:tm - 8],
                            hbar[tm - 8:] + jnp.where(row8 == 7, h_next, 0.0)], axis=0)
    xx = (hbar - h).astype(BF16)
    mu = mu_ref[...].astype(BF16)

    def mix(c):
        return hb + xx * mu[c:c + 1, :]

    xr, xk, xv = mix(0), mix(1), mix(2)
    w_lo = jnp.tanh(_dot(mix(3), w1_ref[...])).astype(BF16)
    a_lo = _dot(mix(4), a1_ref[...]).astype(BF16)
    g_lo = _sigmoid(_dot(mix(5), g1_ref[...])).astype(BF16)
    ones_bd = ones_ref[...]

    def cols_of(grp, part=0):
        return slice(part * d + grp * GROUP_W, part * d + (grp + 1) * GROUP_W)

    def dots(grp):
        cols = cols_of(grp)
        rvk_o[:, cols] = _dot(xr, wrkv_ref[0, :, cols]).astype(BF16)
        rvk_o[:, cols_of(grp, 1)] = _dot(xv, wrkv_ref[2, :, cols]).astype(BF16)
        gate_o[:, cols] = _dot(g_lo, g2_ref[:, cols]).astype(BF16)
        return (_dot(xk, wrkv_ref[1, :, cols]),
                [_dot(w_lo, w2_ref[j, :, cols]) for j in range(2)],
                [_dot(a_lo, a2_ref[j, :, cols]) for j in range(2)])

    def tail(grp, k, w_lora, a_lora):
        cols = cols_of(grp)
        kkr = k * kk_ref[:, cols]
        kkn = kkr * lax.rsqrt(jnp.maximum(_group_head_sums(kkr * kkr, ones_bd), 1e-24))
        rvk_o[:, cols_of(grp, 2)] = kkn.astype(BF16)
        k_b = k * ka_ref[:, cols]
        k_a = k - k_b
        for j, (kb_o, lw_o) in enumerate(((kb0_o, lw0_o), (kb1_o, lw1_o))):
            lw_o[:, cols] = -DECAY_SCALE * _sigmoid(w0_ref[j:j + 1, cols] + w_lora[j])
            a = _sigmoid(a0_ref[j:j + 1, cols] + a_lora[j])
            kb_o[:, cols] = (k_a + k_b * a).astype(BF16)
            kb_o[:, cols_of(grp, 1)] = (kkn * a).astype(BF16)

    pending = dots(0)
    for grp in range(N_GROUPS):
        nxt = dots(grp + 1) if grp + 1 < N_GROUPS else None
        tail(grp, *pending)
        pending = nxt


def _wkv_consts(reverse):
    row = lax.broadcasted_iota(jnp.int32, (CHUNK, GROUP_W), 0)
    s = lax.broadcasted_iota(jnp.int32, (CHUNK, GROUP_W), 1) % HEAD
    ri = lax.broadcasted_iota(jnp.int32, (CHUNK, CHUNK), 0)
    ci = lax.broadcasted_iota(jnp.int32, (CHUNK, CHUNK), 1)
    if reverse:
        strict, incl, tri = s > row, s >= row, ci >= ri
    else:
        strict, incl, tri = s < row, s <= row, ci <= ri
    return dict(strict=strict, incl=incl, eye=s == row, tri=jnp.where(tri, 1.0, 0.0).astype(BF16))


def _wkv_kernel(rvkf, kbf, lwf, rvkb, kbb, lwb, yf_o, yb_o, state, *, cps, wave):
    @pl.when(pl.program_id(1) == 0)
    def _():
        state[...] = jnp.zeros_like(state)

    d_model = lwf.shape[1]
    consts = (_wkv_consts(False), _wkv_consts(True))
    dirs = ((rvkf, kbf, lwf, yf_o), (rvkb, kbb, lwb, yb_o))
    inst = [(d, g) for d in range(2) for g in range(N_GROUPS)]
    n = range(len(inst))
    half_w = GROUP_W // 2
    low_half = lax.broadcasted_iota(jnp.int32, (CHUNK, half_w), 1) < HEAD
    zero_half = jnp.zeros((CHUNK, half_w), BF16)

    def bd(y):
        yb = y.astype(BF16)
        parts = []
        for p in range(GROUP_HEADS):
            side = p // 2
            half = yb[:, side * half_w:(side + 1) * half_w]
            kept = jnp.where(low_half if p % 2 == 0 else ~low_half, half, zero_half)
            parts.append(jnp.concatenate([kept, zero_half] if side == 0 else [zero_half, kept], axis=1))
        return jnp.concatenate(parts, axis=0)

    def tr(y):
        yt = y.T
        return jnp.concatenate([yt[p * HEAD:(p + 1) * HEAD] for p in range(GROUP_HEADS)], axis=1)

    def mm(x, y):
        return _dot(x, bd(y))

    def cat(*xs):
        return jnp.concatenate(xs, axis=0)

    def bf(y):
        return y.astype(BF16)

    def rows_of(d, k):
        j = k if d == 0 else cps - 1 - k
        return slice(j * CHUNK, (j + 1) * CHUNK)

    def cols_of(g, part=0):
        return slice(part * d_model + g * GROUP_W, part * d_model + (g + 1) * GROUP_W)

    def prologue(k, out):
        e = [dict(), dict()]
        for d in range(2):
            lw = dirs[d][2][rows_of(d, k), :]
            hi = lw.astype(BF16)
            lo = (lw - hi.astype(F32)).astype(BF16)
            tri = consts[d]["tri"]
            cum = _dot(tri, hi) + _dot(tri, lo)
            last = cum[0:1, :] if d == 1 else cum[CHUNK - 1:CHUNK, :]
            e[d].update(lw=lw, cum=cum, last=last)
        yield
        for d in range(2):
            e[d].update(e_pos=jnp.exp(e[d]["cum"]), e_neg=jnp.exp(-e[d]["cum"]))
        yield
        for d in range(2):
            e[d].update(e_excl=jnp.exp(e[d]["cum"] - e[d]["lw"]), e_last=jnp.exp(e[d]["last"] - e[d]["cum"]),
                        w_last=jnp.exp(e[d]["last"]))
        yield

        def load(ref_idx, part, d, g):
            return dirs[d][ref_idx][rows_of(d, k), cols_of(g, part)].astype(F32)

        def ex(name, d, g):
            return e[d][name][:, cols_of(g)]

        out["rd"] = [load(0, 0, d, g) * ex("e_pos", d, g) for d, g in inst]
        out["kkd"] = [bf(load(0, 2, d, g) * ex("e_excl", d, g)) for d, g in inst]
        out["v"] = [dirs[d][0][rows_of(d, k), cols_of(g, 1)] for d, g in inst]
        out["w_last"] = [ex("w_last", d, g) for d, g in inst]
        yield
        kd = [load(1, 0, d, g) for d, g in inst]
        out["kdiv_t"] = [bf(tr(kd[i] * ex("e_neg", *inst[i]))) for i in n]
        out["kw_t"] = [bf(tr(kd[i] * ex("e_last", *inst[i]))) for i in n]
        yield
        b = [load(1, 1, d, g) for d, g in inst]
        out["bdiv_t"] = [bf(tr(b[i] * ex("e_neg", *inst[i]))) for i in n]
        out["bw_t"] = [bf(tr(b[i] * ex("e_last", *inst[i]))) for i in n]
        yield

    def stages(wave_chunks, st):
        items = [(k, i) for k in wave_chunks for i in n]
        m = range(len(items))
        c = [consts[inst[i][0]] for _, i in items]

        def pr(name):
            return [pro[k][name][i] for k, i in items]

        rd, kkd, v = pr("rd"), pr("kkd"), pr("v")
        kdiv_t, bdiv_t, kw_t, bw_t, w_last = pr("kdiv_t"), pr("bdiv_t"), pr("kw_t"), pr("bw_t"), pr("w_last")
        x = [cat(kkd[j], bf(rd[j])) for j in m]
        sk = [mm(x[j], kdiv_t[j]) for j in m]
        yield
        sb = [mm(x[j], bdiv_t[j]) for j in m]
        yield
        m_kk = [bf(jnp.where(c[j]["strict"], sk[j][:CHUNK], 0.0)) for j in m]
        a_rk = [bf(jnp.where(c[j]["incl"], sk[j][CHUNK:], 0.0)) for j in m]
        a_rb = [bf(jnp.where(c[j]["incl"], sb[j][CHUNK:], 0.0)) for j in m]
        a = [jnp.where(c[j]["strict"], -sb[j][:CHUNK], 0.0) for j in m]
        tinv = [jnp.where(c[j]["eye"], 1.0, 0.0) + a[j] for j in m]
        ap = [bf(a[j]) for j in m]
        ap = [bf(mm(ap[j], ap[j])) for j in m]
        yield
        for step in range(5):
            if step < 4:
                both = [mm(cat(bf(tinv[j]), ap[j]), ap[j]) for j in m]
                tinv = [tinv[j] + both[j][:CHUNK] for j in m]
                ap = [bf(both[j][CHUNK:]) for j in m]
            else:
                tinv = [tinv[j] + mm(bf(tinv[j]), ap[j]) for j in m]
            yield
        xv = [mm(cat(m_kk[j], a_rk[j], kw_t[j]), v[j]) for j in m]
        yield
        uz = [bf(mm(cat(a_rb[j], bw_t[j]), tinv[j])) for j in m]
        yield
        uzp = [mm(uz[j], kkd[j]) for j in m]
        uzq = [mm(uz[j], xv[j][:CHUNK]) for j in m]
        yield
        rp = [rd[j] - uzp[j][:CHUNK] for j in m]
        g_ = [jnp.where(c[j]["eye"], w_last[j], 0.0) - uzp[j][CHUNK:] for j in m]
        y0 = [xv[j][CHUNK:2 * CHUNK] - uzq[j][:CHUNK] for j in m]
        h_ = [xv[j][2 * CHUNK:] - uzq[j][CHUNK:] for j in m]
        for j, (k, i) in enumerate(items):
            d, g = inst[i]
            both = mm(bf(cat(g_[j], rp[j])), st[i])
            st[i] = both[:CHUNK] + h_[j]
            dirs[d][3][rows_of(d, k), cols_of(g)] = (both[CHUNK:] + y0[j]).astype(BF16)
        yield

    def run_all(gens):
        for gen in gens:
            for _ in gen:
                pass

    pro = [dict() for _ in range(cps)]
    waves = [list(range(w, min(w + wave, cps))) for w in range(0, cps, wave)]
    run_all([prologue(k, pro[k]) for k in waves[0]])
    st = [state[d, g] for d, g in inst]
    for wi, wv in enumerate(waves):
        side = [prologue(k, pro[k]) for k in waves[wi + 1]] if wi + 1 < len(waves) else []
        for _ in stages(wv, st):
            for gen in side:
                next(gen, None)
        run_all(side)
    for i, (d, g) in enumerate(inst):
        state[d, g] = st[i]


def _rwkv_post_kernel(yf_ref, yb_ref, r_ref, v_ref, kd0_ref, kd1_ref, gate_ref, x_ref,
                      lnw_ref, lnb_ref, rk_ref, wo_ref, ones_ref, o_ref):
    ones_bd = ones_ref[...]
    y = yf_ref[...].astype(F32) + yb_ref[...].astype(F32)
    mean = _head_sums(y, ones_bd) * (1.0 / HEAD)
    dlt = y - mean
    var = _head_sums(dlt * dlt, ones_bd) * (1.0 / HEAD)
    yn = dlt * lax.rsqrt(var + GN_EPS) * lnw_ref[...] + lnb_ref[...]
    r = r_ref[...].astype(F32)
    ksum = kd0_ref[...].astype(F32) + kd1_ref[...].astype(F32)
    bonus = _head_sums(r * ksum * rk_ref[...], ones_bd) * v_ref[...].astype(F32)
    out = ((yn + bonus) * gate_ref[...].astype(F32)).astype(BF16)
    o_ref[...] = x_ref[...] + _dot(out, wo_ref[...])


def _rwkv_mix(x2d, g, mu, w_rkv, w_o, w0, w1, w2, a0, a1, a2, g1, g2, k_k, k_a, r_k, ln_w, ln_b,
              batch, seq):
    t, d = x2d.shape
    tm = ROW_TILE
    tiles_per_seq = seq // tm
    lora = w1.shape[-1]
    n_dirs = w1.shape[0]
    row = lambda vec: vec.reshape(1, d)

    w1cat = jnp.concatenate([w1[j] for j in range(n_dirs)], axis=1).astype(BF16)
    a1cat = jnp.concatenate([a1[j] for j in range(n_dirs)], axis=1).astype(BF16)
    w2pad = jnp.stack([jnp.pad(w2[j], ((j * lora, (n_dirs - 1 - j) * lora), (0, 0))) for j in range(n_dirs)]).astype(BF16)
    a2pad = jnp.stack([jnp.pad(a2[j], ((j * lora, (n_dirs - 1 - j) * lora), (0, 0))) for j in range(n_dirs)]).astype(BF16)
    li = jnp.arange(GROUP_W, dtype=jnp.int32) // HEAD
    ones_bd = (li[:, None] == li[None, :]).astype(BF16)

    full = lambda a: pl.BlockSpec(a.shape, lambda i: (0,) * a.ndim)
    tile = lambda part: pl.BlockSpec((tm, d), lambda i: (i, part))
    wide = lambda parts: pl.BlockSpec((tm, parts * d), lambda i: (i, 0))
    n8 = t // 8
    prev8 = pl.BlockSpec((8, d), lambda i: (jnp.maximum(i * (tm // 8) - 1, 0), 0))
    next8 = pl.BlockSpec((8, d), lambda i: (jnp.minimum((i + 1) * (tm // 8), n8 - 1), 0))
    ti = jnp.arange(tm, dtype=jnp.int32)
    sbar = jnp.where(jnp.abs(ti[:, None] - ti[None, :]) == 1, 0.5, 0.0).astype(BF16)
    consts = (row(g), mu, sbar, w_rkv.astype(BF16), w1cat, w2pad, a1cat, a2pad, g1.astype(BF16), g2.astype(BF16),
              w0, a0, row(k_k), row(k_a), ones_bd)
    bf_out = lambda parts: jax.ShapeDtypeStruct((t, parts * d), BF16)
    f32_out = jax.ShapeDtypeStruct((t, d), F32)
    rvk, gate, kb0, kb1, lw0, lw1 = pl.pallas_call(
        functools.partial(_rwkv_prep_kernel, tiles_per_seq=tiles_per_seq),
        grid=(t // tm,),
        in_specs=[tile(0), prev8, next8] + [full(a) for a in consts],
        out_specs=[wide(3), tile(0), wide(2), wide(2), tile(0), tile(0)],
        out_shape=[bf_out(3), bf_out(1), bf_out(2), bf_out(2), f32_out, f32_out],
        compiler_params=_params(),
        name="rwkv_prep",
    )(x2d, x2d, x2d, *consts)

    cps, wave = 4, 2
    nc = seq // (CHUNK * cps)
    fwd = lambda parts: pl.BlockSpec((CHUNK * cps, parts * d), lambda b, c: (b * nc + c, 0))
    bwd = lambda parts: pl.BlockSpec((CHUNK * cps, parts * d), lambda b, c: (b * nc + nc - 1 - c, 0))
    yf, yb = pl.pallas_call(
        functools.partial(_wkv_kernel, cps=cps, wave=wave),
        grid=(batch, nc),
        in_specs=[fwd(3), fwd(2), fwd(1), bwd(3), bwd(2), bwd(1)],
        out_specs=[fwd(1), bwd(1)],
        out_shape=[bf_out(1), bf_out(1)],
        scratch_shapes=[pltpu.VMEM((2, N_GROUPS, HEAD, GROUP_W), F32)],
        compiler_params=_params(),
        name="wkv7_chunked",
    )(rvk, kb0, lw0, rvk, kb1, lw1)

    consts = (row(ln_w), row(ln_b), row(r_k), w_o.astype(BF16), ones_bd)
    return pl.pallas_call(
        _rwkv_post_kernel,
        grid=(t // tm,),
        in_specs=[tile(0)] * 2 + [tile(0), tile(1), tile(0), tile(0), tile(0), tile(0)] + [full(a) for a in consts],
        out_specs=tile(0),
        out_shape=f32_out,
        compiler_params=_params(),
        name="rwkv_post",
    )(yf, yb, rvk, rvk, kb0, kb1, gate, x2d, *consts)


def kernel(x, norm_mix_g, norm_ffn_g, norm_final_g, fno_w_out, rwkv_mu, rwkv_w_rkv, rwkv_w_o, rwkv_w0, rwkv_w1, rwkv_w2, rwkv_a0, rwkv_a1, rwkv_a2, rwkv_g1, rwkv_g2, rwkv_k_k, rwkv_k_a, rwkv_r_k, rwkv_ln_w, rwkv_ln_b, ffn_w_gate, ffn_w_up, ffn_w_down):
    batch, seq, d = x.shape
    depth = norm_mix_g.shape[0]
    h = x.reshape(batch * seq, d)
    for i in range(depth):
        j = i // 2
        x_index_map = None
        if i % 2 == 0:
            h = _fourier_mix(h, norm_mix_g[i], fno_w_out[j], batch, seq)
            x_index_map = functools.partial(_mirror_tile_map, seq=seq)
        else:
            h = _rwkv_mix(h, norm_mix_g[i], rwkv_mu[j], rwkv_w_rkv[j], rwkv_w_o[j], rwkv_w0[j], rwkv_w1[j],
                          rwkv_w2[j], rwkv_a0[j], rwkv_a1[j], rwkv_a2[j], rwkv_g1[j], rwkv_g2[j],
                          rwkv_k_k[j], rwkv_k_a[j], rwkv_r_k[j], rwkv_ln_w[j], rwkv_ln_b[j], batch, seq)
        h = _ffn(h, norm_ffn_g[i], ffn_w_gate[i], ffn_w_up[i], ffn_w_down[i], norm_final_g,
                 final_norm=(i == depth - 1), x_index_map=x_index_map)
    return h.reshape(batch, seq, d)
```

```python
import functools
import math

import jax
import jax.numpy as jnp
from jax import lax
from jax.experimental import pallas as pl
from jax.experimental.pallas import tpu as pltpu

F32 = jnp.float32
BF16 = jnp.bfloat16

D_MODEL = 1024
HEAD = 64
GROUP_HEADS = 4
GROUP_W = GROUP_HEADS * HEAD
N_GROUPS = D_MODEL // GROUP_W
CHUNK = 64
ROW_TILE = 512
FOURIER_GROUPS = 4
FOURIER_GW = D_MODEL // FOURIER_GROUPS
FOURIER_TILE = 512
BF16_ROWS = 16
RMS_EPS = 1e-6
GN_EPS = HEAD * 1e-5
DECAY_SCALE = math.exp(-0.5)
VMEM_LIMIT = 58 * 1024 * 1024


def _params():
    return pltpu.CompilerParams(vmem_limit_bytes=VMEM_LIMIT)


def _dot(a, b):
    return jnp.dot(a, b, preferred_element_type=F32)


def _rms(x, g):
    return x * lax.rsqrt(jnp.mean(x * x, axis=-1, keepdims=True) + RMS_EPS) * g


def _sigmoid(z):
    return 1.0 / (1.0 + jnp.exp(-z))


def _group_head_sums(ug, ones_bd):
    return _dot(ug.astype(BF16), ones_bd)


def _head_sums(u, ones_bd):
    return jnp.concatenate([_group_head_sums(u[:, g * GROUP_W:(g + 1) * GROUP_W], ones_bd)
                            for g in range(N_GROUPS)], axis=1)


def _fourier_kernel(x_ref, g_ref, ccsc_ref, ch_ref, sh_ref, rev_ref, w_ref, o_ref, zc, zs):
    i = pl.program_id(1)
    ft = FOURIER_TILE
    seq = x_ref.shape[1]
    n_tiles = seq // ft

    @pl.when(i == 0)
    def _():
        for t in range(n_tiles):
            rows = slice(t * ft, (t + 1) * ft)
            h = _rms(x_ref[0, rows, :], g_ref[...]).astype(BF16)
            for g in range(FOURIER_GROUPS):
                cols = slice(g * FOURIER_GW, (g + 1) * FOURIER_GW)
                z = _dot(h[:, cols], ccsc_ref[...])
                zc[rows, cols] = z[:, :FOURIER_GW].astype(BF16)
                zs[rows, cols] = z[:, FOURIER_GW:].astype(BF16)

    t1 = _dot(ch_ref[0], zc[...])
    t2 = _dot(sh_ref[0], zs[...])
    w = w_ref[...]
    top = (t1[:ft] - t2[:ft]).astype(BF16)
    r_top = pl.multiple_of(i * ft, ft)
    o_ref[0, 0, 0] = x_ref[0, pl.ds(r_top, ft), :] + _dot(top, w)
    bot = _dot(rev_ref[...], (t1 + t2).astype(BF16)).astype(BF16)
    r_bot = pl.multiple_of((n_tiles - 1 - i) * ft, ft)
    o_ref[0, 1, 0] = x_ref[0, pl.ds(r_bot, ft), :] + _dot(bot, w)


def _fourier_mix(x2d, g, w_out, batch, seq):
    t, d = x2d.shape
    ft = FOURIER_TILE
    ft_pad = ft + BF16_ROWS
    nh = seq // (2 * ft)
    n = jnp.arange(FOURIER_GW, dtype=jnp.int32)
    ang = ((n[:, None] * n[None, :]) % FOURIER_GW).astype(F32) * (2.0 * math.pi / FOURIER_GW)
    ccsc = (jnp.concatenate([jnp.cos(ang), jnp.sin(ang)], axis=1) * (FOURIER_GW ** -0.5)).astype(BF16)
    n = jnp.arange(seq, dtype=jnp.int32)
    blk = 32
    n_hi = -(-((nh - 1) * ft + ft_pad) // blk)

    def table(kvec):
        ang = ((kvec[:, None] * n[None, :]) % seq).astype(F32) * (2.0 * math.pi / seq)
        return jnp.cos(ang), jnp.sin(ang)

    c_hi, s_hi = (tab[:, None, :] for tab in table(jnp.arange(n_hi, dtype=jnp.int32) * blk))
    c_lo, s_lo = (tab[None, :, :] for tab in table(jnp.arange(blk, dtype=jnp.int32)))
    cos_k = ((c_hi * c_lo - s_hi * s_lo) * (seq ** -0.5)).reshape(n_hi * blk, seq)
    sin_k = ((s_hi * c_lo + c_hi * s_lo) * (seq ** -0.5)).reshape(n_hi * blk, seq)
    ch = jnp.stack([cos_k[i * ft:i * ft + ft_pad] for i in range(nh)]).astype(BF16)
    sh = jnp.stack([sin_k[i * ft:i * ft + ft_pad] for i in range(nh)]).astype(BF16)
    rev = (jnp.arange(ft, dtype=jnp.int32)[:, None] + jnp.arange(ft_pad, dtype=jnp.int32)[None, :] == ft).astype(BF16)

    const2 = lambda a: pl.BlockSpec(a.shape, lambda b, i: (0,) * a.ndim)
    g2d = g.reshape(1, d)
    w_bf = w_out.astype(BF16)
    out = pl.pallas_call(
        _fourier_kernel,
        grid=(batch, nh),
        in_specs=[pl.BlockSpec((1, seq, d), lambda b, i: (b, 0, 0)),
                  const2(g2d), const2(ccsc),
                  pl.BlockSpec((1, ft_pad, seq), lambda b, i: (i, 0, 0)),
                  pl.BlockSpec((1, ft_pad, seq), lambda b, i: (i, 0, 0)),
                  const2(rev), const2(w_bf)],
        out_specs=pl.BlockSpec((1, 2, 1, ft, d), lambda b, i: (b, 0, i, 0, 0)),
        out_shape=jax.ShapeDtypeStruct((batch, 2, nh, ft, d), F32),
        scratch_shapes=[pltpu.VMEM((seq, d), BF16), pltpu.VMEM((seq, d), BF16)],
        compiler_params=_params(),
        name="fourier_mix",
    )(x2d.reshape(batch, seq, d), g2d, ccsc, ch, sh, rev, w_bf)
    return out.reshape(t, d)


def _mirror_tile_map(tm, seq):
    per = FOURIER_TILE // tm
    n_tiles = seq // FOURIER_TILE
    nh = n_tiles // 2
    tiles_per_seq = seq // tm

    def index_map(i):
        b, u = i // tiles_per_seq, i % tiles_per_seq
        j, sub = u // per, u % per
        stored = jnp.where(j < nh, j, nh + n_tiles - 1 - j)
        return (b * tiles_per_seq + stored * per + sub, 0)

    return index_map


def _ffn_kernel(x_ref, g_ref, wg_ref, wu_ref, wd_ref, gf_ref, o_ref, *, final_norm):
    x = x_ref[...]
    h = _rms(x, g_ref[...]).astype(BF16)
    gate = _dot(h, wg_ref[...])
    up = _dot(h, wu_ref[...])
    act = (gate * _sigmoid(gate) * up).astype(BF16)
    y = x + _dot(act, wd_ref[...])
    if final_norm:
        y = _rms(y, gf_ref[...])
    o_ref[...] = y


def _ffn(x2d, g, w_gate, w_up, w_down, g_final, final_norm, x_index_map=None):
    t, d = x2d.shape
    f = w_gate.shape[1]
    tm = ROW_TILE
    if x_index_map is None:
        x_index_map = lambda i: (i, 0)
    else:
        x_index_map = x_index_map(tm)
    resident = lambda shape: pl.BlockSpec(shape, lambda i: (0, 0), pipeline_mode=pl.Buffered(1))
    return pl.pallas_call(
        functools.partial(_ffn_kernel, final_norm=final_norm),
        grid=(t // tm,),
        in_specs=[pl.BlockSpec((tm, d), x_index_map),
                  pl.BlockSpec((1, d), lambda i: (0, 0)),
                  resident((d, f)),
                  resident((d, f)),
                  resident((f, d)),
                  pl.BlockSpec((1, d), lambda i: (0, 0))],
        out_specs=pl.BlockSpec((tm, d), lambda i: (i, 0)),
        out_shape=jax.ShapeDtypeStruct((t, d), F32),
        compiler_params=_params(),
        name="swiglu_ffn",
    )(x2d, g.reshape(1, d), w_gate.astype(BF16), w_up.astype(BF16), w_down.astype(BF16),
      g_final.reshape(1, d))


def _rwkv_prep_kernel(x_ref, xp_ref, xn_ref, g_ref, mu_ref, sbar_ref, wrkv_ref, lora_ref, w2_ref, a2_ref,
                      g2_ref, w0_ref, a0_ref, kk_ref, ka_ref, rk_ref, ones_ref,
                      rvk_o, gate_o, bonus_o, kb0_o, kb1_o, lw0_o, lw1_o, *, tiles_per_seq):
    i = pl.program_id(0)
    tm, d = x_ref.shape
    g = g_ref[...]
    h = _rms(x_ref[...], g)
    hb = h.astype(BF16)
    pos = i % tiles_per_seq
    keep_prev = jnp.where(pos == 0, 0.0, 0.5)
    keep_next = jnp.where(pos == tiles_per_seq - 1, 0.0, 0.5)
    h_prev = _rms(xp_ref[...], g)[7:8, :] * keep_prev
    h_next = _rms(xn_ref[...], g)[0:1, :] * keep_next
    hbar = _dot(sbar_ref[...], hb)
    row8 = lax.broadcasted_iota(jnp.int32, (8, d), 0)
    hbar = jnp.concatenate([hbar[:8] + jnp.where(row8 == 0, h_prev, 0.0), hbar[8:tm - 8],
                            hbar[tm - 8:] + jnp.where(row8 == 7, h_next, 0.0)], axis=0)
    xx = (hbar - h).astype(BF16)
    mu = mu_ref[...].astype(BF16)

    def mix(c):
        return hb + xx * mu[c:c + 1, :]

    xr, xk, xv = mix(0), mix(1), mix(2)
    lo = _dot(jnp.concatenate([hb, xx], axis=1), lora_ref[...])
    nl = lo.shape[1] // 3
    w_lo = jnp.tanh(lo[:, :nl]).astype(BF16)
    a_lo = lo[:, nl:2 * nl].astype(BF16)
    g_lo = _sigmoid(lo[:, 2 * nl:]).astype(BF16)
    ones_bd = ones_ref[...]

    def cols_of(grp, part=0):
        return slice(part * d + grp * GROUP_W, part * d + (grp + 1) * GROUP_W)

    def dots(grp):
        cols = cols_of(grp)
        r = _dot(xr, wrkv_ref[0, :, cols])
        v = _dot(xv, wrkv_ref[2, :, cols])
        rvk_o[:, cols] = r.astype(BF16)
        rvk_o[:, cols_of(grp, 1)] = v.astype(BF16)
        gate_o[:, cols] = _dot(g_lo, g2_ref[:, cols]).astype(BF16)
        return (r, _dot(xk, wrkv_ref[1, :, cols]), v,
                [_dot(w_lo, w2_ref[j, :, cols]) for j in range(2)],
                [_dot(a_lo, a2_ref[j, :, cols]) for j in range(2)])

    def tail(grp, r, k, v, w_lora, a_lora):
        cols = cols_of(grp)
        kkr = k * kk_ref[:, cols]
        kkn = kkr * lax.rsqrt(jnp.maximum(_group_head_sums(kkr * kkr, ones_bd), 1e-24))
        rvk_o[:, cols_of(grp, 2)] = kkn.astype(BF16)
        k_b = k * ka_ref[:, cols]
        k_a = k - k_b
        a_sum = None
        for j, (kb_o, lw_o) in enumerate(((kb0_o, lw0_o), (kb1_o, lw1_o))):
            lw_o[:, cols] = -DECAY_SCALE * _sigmoid(w0_ref[j:j + 1, cols] + w_lora[j])
            a = _sigmoid(a0_ref[j:j + 1, cols] + a_lora[j])
            kb_o[:, cols] = (k_a + k_b * a).astype(BF16)
            kb_o[:, cols_of(grp, 1)] = (kkn * a).astype(BF16)
            a_sum = a if a_sum is None else a_sum + a
        k_sum = 2.0 * k_a + k_b * a_sum
        bonus_o[:, cols] = (_group_head_sums(r * k_sum * rk_ref[:, cols], ones_bd) * v).astype(BF16)

    pending = dots(0)
    for grp in range(N_GROUPS):
        nxt = dots(grp + 1) if grp + 1 < N_GROUPS else None
        tail(grp, *pending)
        pending = nxt


def _wkv_consts(reverse):
    row = lax.broadcasted_iota(jnp.int32, (CHUNK, GROUP_W), 0)
    s = lax.broadcasted_iota(jnp.int32, (CHUNK, GROUP_W), 1) % HEAD
    ri = lax.broadcasted_iota(jnp.int32, (CHUNK, CHUNK), 0)
    ci = lax.broadcasted_iota(jnp.int32, (CHUNK, CHUNK), 1)
    if reverse:
        strict, incl, tri = s > row, s >= row, ci >= ri
    else:
        strict, incl, tri = s < row, s <= row, ci <= ri
    return dict(strict=strict, incl=incl, eye=s == row, tri=jnp.where(tri, 1.0, 0.0).astype(BF16))


def _wkv_kernel(rvkf, kbf, lwf, rvkb, kbb, lwb, yf_o, yb_o, state, *, cps, wave):
    @pl.when(pl.program_id(1) == 0)
    def _():
        state[...] = jnp.zeros_like(state)

    d_model = lwf.shape[1]
    consts = (_wkv_consts(False), _wkv_consts(True))
    dirs = ((rvkf, kbf, lwf, yf_o), (rvkb, kbb, lwb, yb_o))
    inst = [(d, g) for d in range(2) for g in range(N_GROUPS)]
    n = range(len(inst))
    half_w = GROUP_W // 2
    low_half = lax.broadcasted_iota(jnp.int32, (CHUNK, half_w), 1) < HEAD
    zero_half = jnp.zeros((CHUNK, half_w), BF16)

    def bd(y):
        yb = y.astype(BF16)
        parts = []
        for p in range(GROUP_HEADS):
            side = p // 2
            half = yb[:, side * half_w:(side + 1) * half_w]
            kept = jnp.where(low_half if p % 2 == 0 else ~low_half, half, zero_half)
            parts.append(jnp.concatenate([kept, zero_half] if side == 0 else [zero_half, kept], axis=1))
        return jnp.concatenate(parts, axis=0)

    def tr(y):
        yt = y.T
        return jnp.concatenate([yt[p * HEAD:(p + 1) * HEAD] for p in range(GROUP_HEADS)], axis=1)

    def mm(x, y):
        return _dot(x, bd(y))

    def cat(*xs):
        return jnp.concatenate(xs, axis=0)

    def bf(y):
        return y.astype(BF16)

    def rows_of(d, k):
        j = k if d == 0 else cps - 1 - k
        return slice(j * CHUNK, (j + 1) * CHUNK)

    def cols_of(g, part=0):
        return slice(part * d_model + g * GROUP_W, part * d_model + (g + 1) * GROUP_W)

    def prologue(k, out):
        e = [dict(), dict()]
        for d in range(2):
            lw = dirs[d][2][rows_of(d, k), :]
            hi = lw.astype(BF16)
            lo = (lw - hi.astype(F32)).astype(BF16)
            tri = consts[d]["tri"]
            cum = _dot(tri, hi) + _dot(tri, lo)
            last = cum[0:1, :] if d == 1 else cum[CHUNK - 1:CHUNK, :]
            e[d].update(lw=lw, cum=cum, last=last)
        yield
        for d in range(2):
            e[d].update(e_pos=jnp.exp(e[d]["cum"]), e_neg=jnp.exp(-e[d]["cum"]))
        yield
        for d in range(2):
            e[d].update(e_excl=jnp.exp(e[d]["cum"] - e[d]["lw"]), e_last=jnp.exp(e[d]["last"] - e[d]["cum"]),
                        w_last=jnp.exp(e[d]["last"]))
        yield

        def load(ref_idx, part, d, g):
            return dirs[d][ref_idx][rows_of(d, k), cols_of(g, part)].astype(F32)

        def ex(name, d, g):
            return e[d][name][:, cols_of(g)]

        out["rd"] = [load(0, 0, d, g) * ex("e_pos", d, g) for d, g in inst]
        out["kkd"] = [bf(load(0, 2, d, g) * ex("e_excl", d, g)) for d, g in inst]
        out["v"] = [dirs[d][0][rows_of(d, k), cols_of(g, 1)] for d, g in inst]
        out["w_last"] = [ex("w_last", d, g) for d, g in inst]
        yield
        kd = [load(1, 0, d, g) for d, g in inst]
        out["kdiv_t"] = [bf(tr(kd[i] * ex("e_neg", *inst[i]))) for i in n]
        out["kw_t"] = [bf(tr(kd[i] * ex("e_last", *inst[i]))) for i in n]
        yield
        b = [load(1, 1, d, g) for d, g in inst]
        out["bdiv_t"] = [bf(tr(b[i] * ex("e_neg", *inst[i]))) for i in n]
        out["bw_t"] = [bf(tr(b[i] * ex("e_last", *inst[i]))) for i in n]
        yield

    def stages(wave_chunks, st):
        items = [(k, i) for k in wave_chunks for i in n]
        m = range(len(items))
        c = [consts[inst[i][0]] for _, i in items]

        def pr(name):
            return [pro[k][name][i] for k, i in items]

        rd, kkd, v = pr("rd"), pr("kkd"), pr("v")
        kdiv_t, bdiv_t, kw_t, bw_t, w_last = pr("kdiv_t"), pr("bdiv_t"), pr("kw_t"), pr("bw_t"), pr("w_last")
        x = [cat(kkd[j], bf(rd[j])) for j in m]
        sk = [mm(x[j], kdiv_t[j]) for j in m]
        yield
        sb = [mm(x[j], bdiv_t[j]) for j in m]
        yield
        m_kk = [bf(jnp.where(c[j]["strict"], sk[j][:CHUNK], 0.0)) for j in m]
        a_rk = [bf(jnp.where(c[j]["incl"], sk[j][CHUNK:], 0.0)) for j in m]
        a_rb = [bf(jnp.where(c[j]["incl"], sb[j][CHUNK:], 0.0)) for j in m]
        a = [jnp.where(c[j]["strict"], -sb[j][:CHUNK], 0.0) for j in m]
        tinv = [jnp.where(c[j]["eye"], 1.0, 0.0) + a[j] for j in m]
        ap = [bf(a[j]) for j in m]
        ap = [bf(mm(ap[j], ap[j])) for j in m]
        yield
        for step in range(5):
            if step < 4:
                both = [mm(cat(bf(tinv[j]), ap[j]), ap[j]) for j in m]
                tinv = [tinv[j] + both[j][:CHUNK] for j in m]
                ap = [bf(both[j][CHUNK:]) for j in m]
            else:
                tinv = [tinv[j] + mm(bf(tinv[j]), ap[j]) for j in m]
            yield
        xv = [mm(cat(m_kk[j], a_rk[j], kw_t[j]), v[j]) for j in m]
        yield
        uz = [bf(mm(cat(a_rb[j], bw_t[j]), tinv[j])) for j in m]
        yield
        uzp = [mm(uz[j], kkd[j]) for j in m]
        uzq = [mm(uz[j], xv[j][:CHUNK]) for j in m]
        yield
        rp = [rd[j] - uzp[j][:CHUNK] for j in m]
        g_ = [jnp.where(c[j]["eye"], w_last[j], 0.0) - uzp[j][CHUNK:] for j in m]
        y0 = [xv[j][CHUNK:2 * CHUNK] - uzq[j][:CHUNK] for j in m]
        h_ = [xv[j][2 * CHUNK:] - uzq[j][CHUNK:] for j in m]
        for j, (k, i) in enumerate(items):
            d, g = inst[i]
            both = mm(bf(cat(g_[j], rp[j])), st[i])
            st[i] = both[:CHUNK] + h_[j]
            dirs[d][3][rows_of(d, k), cols_of(g)] = (both[CHUNK:] + y0[j]).astype(BF16)
        yield

    def run_all(gens):
        for gen in gens:
            for _ in gen:
                pass

    pro = [dict() for _ in range(cps)]
    waves = [list(range(w, min(w + wave, cps))) for w in range(0, cps, wave)]
    run_all([prologue(k, pro[k]) for k in waves[0]])
    st = [state[d, g] for d, g in inst]
    for wi, wv in enumerate(waves):
        side = [prologue(k, pro[k]) for k in waves[wi + 1]] if wi + 1 < len(waves) else []
        for _ in stages(wv, st):
            for gen in side:
                next(gen, None)
        run_all(side)
    for i, (d, g) in enumerate(inst):
        state[d, g] = st[i]


def _rwkv_post_kernel(yf_ref, yb_ref, bonus_ref, gate_ref, x_ref, lnw_ref, lnb_ref, wo_ref, ones_ref, o_ref):
    ones_bd = ones_ref[...]
    y = yf_ref[...].astype(F32) + yb_ref[...].astype(F32)
    mean = _head_sums(y, ones_bd) * (1.0 / HEAD)
    dlt = y - mean
    var = _head_sums(dlt * dlt, ones_bd) * (1.0 / HEAD)
    yn = dlt * lax.rsqrt(var + GN_EPS) * lnw_ref[...] + lnb_ref[...]
    out = ((yn + bonus_ref[...].astype(F32)) * gate_ref[...].astype(F32)).astype(BF16)
    o_ref[...] = x_ref[...] + _dot(out, wo_ref[...])


def _rwkv_mix(x2d, g, mu, w_rkv, w_o, w0, w1, w2, a0, a1, a2, g1, g2, k_k, k_a, r_k, ln_w, ln_b,
              batch, seq):
    t, d = x2d.shape
    tm = ROW_TILE
    tiles_per_seq = seq // tm
    lora = w1.shape[-1]
    n_dirs = w1.shape[0]
    row = lambda vec: vec.reshape(1, d)

    w1cat = jnp.concatenate([w1[j] for j in range(n_dirs)], axis=1)
    a1cat = jnp.concatenate([a1[j] for j in range(n_dirs)], axis=1)
    lora_in = jnp.concatenate([w1cat, a1cat, g1], axis=1)
    lora_mu = jnp.concatenate([mu[3][:, None] * w1cat, mu[4][:, None] * a1cat, mu[5][:, None] * g1], axis=1)
    lora_cat = jnp.concatenate([lora_in, lora_mu], axis=0).astype(BF16)
    w2pad = jnp.stack([jnp.pad(w2[j], ((j * lora, (n_dirs - 1 - j) * lora), (0, 0))) for j in range(n_dirs)]).astype(BF16)
    a2pad = jnp.stack([jnp.pad(a2[j], ((j * lora, (n_dirs - 1 - j) * lora), (0, 0))) for j in range(n_dirs)]).astype(BF16)
    li = jnp.arange(GROUP_W, dtype=jnp.int32) // HEAD
    ones_bd = (li[:, None] == li[None, :]).astype(BF16)

    full = lambda a: pl.BlockSpec(a.shape, lambda i: (0,) * a.ndim)
    tile = lambda part: pl.BlockSpec((tm, d), lambda i: (i, part))
    wide = lambda parts: pl.BlockSpec((tm, parts * d), lambda i: (i, 0))
    n8 = t // 8
    prev8 = pl.BlockSpec((8, d), lambda i: (jnp.maximum(i * (tm // 8) - 1, 0), 0))
    next8 = pl.BlockSpec((8, d), lambda i: (jnp.minimum((i + 1) * (tm // 8), n8 - 1), 0))
    ti = jnp.arange(tm, dtype=jnp.int32)
    sbar = jnp.where(jnp.abs(ti[:, None] - ti[None, :]) == 1, 0.5, 0.0).astype(BF16)
    consts = (row(g), mu, sbar, w_rkv.astype(BF16), lora_cat, w2pad, a2pad, g2.astype(BF16),
              w0, a0, row(k_k), row(k_a), row(r_k), ones_bd)
    bf_out = lambda parts: jax.ShapeDtypeStruct((t, parts * d), BF16)
    f32_out = jax.ShapeDtypeStruct((t, d), F32)
    rvk, gate, bonus, kb0, kb1, lw0, lw1 = pl.pallas_call(
        functools.partial(_rwkv_prep_kernel, tiles_per_seq=tiles_per_seq),
        grid=(t // tm,),
        in_specs=[tile(0), prev8, next8] + [full(a) for a in consts],
        out_specs=[wide(3), tile(0), tile(0), wide(2), wide(2), tile(0), tile(0)],
        out_shape=[bf_out(3), bf_out(1), bf_out(1), bf_out(2), bf_out(2), f32_out, f32_out],
        compiler_params=_params(),
        name="rwkv_prep",
    )(x2d, x2d, x2d, *consts)

    cps, wave = 4, 2
    nc = seq // (CHUNK * cps)
    fwd = lambda parts: pl.BlockSpec((CHUNK * cps, parts * d), lambda b, c: (b * nc + c, 0))
    bwd = lambda parts: pl.BlockSpec((CHUNK * cps, parts * d), lambda b, c: (b * nc + nc - 1 - c, 0))
    yf, yb = pl.pallas_call(
        functools.partial(_wkv_kernel, cps=cps, wave=wave),
        grid=(batch, nc),
        in_specs=[fwd(3), fwd(2), fwd(1), bwd(3), bwd(2), bwd(1)],
        out_specs=[fwd(1), bwd(1)],
        out_shape=[bf_out(1), bf_out(1)],
        scratch_shapes=[pltpu.VMEM((2, N_GROUPS, HEAD, GROUP_W), F32)],
        compiler_params=_params(),
        name="wkv7_chunked",
    )(rvk, kb0, lw0, rvk, kb1, lw1)

    consts = (row(ln_w), row(ln_b), w_o.astype(BF16), ones_bd)
    return pl.pallas_call(
        _rwkv_post_kernel,
        grid=(t // tm,),
        in_specs=[tile(0)] * 5 + [full(a) for a in consts],
        out_specs=tile(0),
        out_shape=f32_out,
        compiler_params=_params(),
        name="rwkv_post",
    )(yf, yb, bonus, gate, x2d, *consts)


def kernel(x, norm_mix_g, norm_ffn_g, norm_final_g, fno_w_out, rwkv_mu, rwkv_w_rkv, rwkv_w_o, rwkv_w0, rwkv_w1, rwkv_w2, rwkv_a0, rwkv_a1, rwkv_a2, rwkv_g1, rwkv_g2, rwkv_k_k, rwkv_k_a, rwkv_r_k, rwkv_ln_w, rwkv_ln_b, ffn_w_gate, ffn_w_up, ffn_w_down):
    batch, seq, d = x.shape
    depth = norm_mix_g.shape[0]
    h = x.reshape(batch * seq, d)
    for i in range(depth):
        j = i // 2
        x_index_map = None
        if i % 2 == 0:
            h = _fourier_mix(h, norm_mix_g[i], fno_w_out[j], batch, seq)
            x_index_map = functools.partial(_mirror_tile_map, seq=seq)
        else:
            h = _rwkv_mix(h, norm_mix_g[i], rwkv_mu[j], rwkv_w_rkv[j], rwkv_w_o[j], rwkv_w0[j], rwkv_w1[j],
                          rwkv_w2[j], rwkv_a0[j], rwkv_a1[j], rwkv_a2[j], rwkv_g1[j], rwkv_g2[j],
                          rwkv_k_k[j], rwkv_k_a[j], rwkv_r_k[j], rwkv_ln_w[j], rwkv_ln_b[j], batch, seq)
        h = _ffn(h, norm_ffn_g[i], ffn_w_gate[i], ffn_w_up[i], ffn_w_down[i], norm_final_g,
                 final_norm=(i == depth - 1), x_index_map=x_index_map)
    return h.reshape(batch, seq, d)
```

```python
import functools
import math

import jax
import jax.numpy as jnp
from jax import lax
from jax.experimental import pallas as pl
from jax.experimental.pallas import tpu as pltpu

F32 = jnp.float32
BF16 = jnp.bfloat16

D_MODEL = 1024
HEAD = 64
GROUP_HEADS = 4
GROUP_W = GROUP_HEADS * HEAD
N_GROUPS = D_MODEL // GROUP_W
CHUNK = 64
ROW_TILE = 512
FOURIER_GROUPS = 4
FOURIER_GW = D_MODEL // FOURIER_GROUPS
FOURIER_TILE = 512
BF16_ROWS = 16
RMS_EPS = 1e-6
GN_EPS = HEAD * 1e-5
DECAY_SCALE = math.exp(-0.5)
VMEM_LIMIT = 58 * 1024 * 1024


def _params():
    return pltpu.CompilerParams(vmem_limit_bytes=VMEM_LIMIT)


def _dot(a, b):
    return jnp.dot(a, b, preferred_element_type=F32)


def _rms(x, g):
    return x * lax.rsqrt(jnp.mean(x * x, axis=-1, keepdims=True) + RMS_EPS) * g


def _sigmoid(z):
    return 1.0 / (1.0 + jnp.exp(-z))


def _group_head_sums(ug, ones_bd):
    return _dot(ug.astype(BF16), ones_bd)


def _head_sums(u, ones_bd):
    return jnp.concatenate([_group_head_sums(u[:, g * GROUP_W:(g + 1) * GROUP_W], ones_bd)
                            for g in range(N_GROUPS)], axis=1)


def _fourier_kernel(x_ref, g_ref, ccsc_ref, ch_ref, sh_ref, rev_ref, w_ref, o_ref, zc, zs):
    i = pl.program_id(1)
    ft = FOURIER_TILE
    seq = x_ref.shape[1]
    n_tiles = seq // ft

    @pl.when(i == 0)
    def _():
        for t in range(n_tiles):
            rows = slice(t * ft, (t + 1) * ft)
            h = _rms(x_ref[0, rows, :], g_ref[...]).astype(BF16)
            for g in range(FOURIER_GROUPS):
                cols = slice(g * FOURIER_GW, (g + 1) * FOURIER_GW)
                z = _dot(h[:, cols], ccsc_ref[...])
                zc[rows, cols] = z[:, :FOURIER_GW].astype(BF16)
                zs[rows, cols] = z[:, FOURIER_GW:].astype(BF16)

    t1 = _dot(ch_ref[0], zc[...])
    t2 = _dot(sh_ref[0], zs[...])
    w = w_ref[...]
    top = (t1[:ft] - t2[:ft]).astype(BF16)
    r_top = pl.multiple_of(i * ft, ft)
    o_ref[0, 0, 0] = x_ref[0, pl.ds(r_top, ft), :] + _dot(top, w)
    bot = _dot(rev_ref[...], (t1 + t2).astype(BF16)).astype(BF16)
    r_bot = pl.multiple_of((n_tiles - 1 - i) * ft, ft)
    o_ref[0, 1, 0] = x_ref[0, pl.ds(r_bot, ft), :] + _dot(bot, w)


def _fourier_mix(x2d, g, w_out, batch, seq):
    t, d = x2d.shape
    ft = FOURIER_TILE
    ft_pad = ft + BF16_ROWS
    nh = seq // (2 * ft)
    n = jnp.arange(FOURIER_GW, dtype=jnp.int32)
    ang = ((n[:, None] * n[None, :]) % FOURIER_GW).astype(F32) * (2.0 * math.pi / FOURIER_GW)
    ccsc = (jnp.concatenate([jnp.cos(ang), jnp.sin(ang)], axis=1) * (FOURIER_GW ** -0.5)).astype(BF16)
    n = jnp.arange(seq, dtype=jnp.int32)
    blk = 32
    n_hi = -(-((nh - 1) * ft + ft_pad) // blk)

    def table(kvec):
        ang = ((kvec[:, None] * n[None, :]) % seq).astype(F32) * (2.0 * math.pi / seq)
        return jnp.cos(ang), jnp.sin(ang)

    c_hi, s_hi = (tab[:, None, :] for tab in table(jnp.arange(n_hi, dtype=jnp.int32) * blk))
    c_lo, s_lo = (tab[None, :, :] for tab in table(jnp.arange(blk, dtype=jnp.int32)))
    cos_k = ((c_hi * c_lo - s_hi * s_lo) * (seq ** -0.5)).reshape(n_hi * blk, seq)
    sin_k = ((s_hi * c_lo + c_hi * s_lo) * (seq ** -0.5)).reshape(n_hi * blk, seq)
    ch = jnp.stack([cos_k[i * ft:i * ft + ft_pad] for i in range(nh)]).astype(BF16)
    sh = jnp.stack([sin_k[i * ft:i * ft + ft_pad] for i in range(nh)]).astype(BF16)
    rev = (jnp.arange(ft, dtype=jnp.int32)[:, None] + jnp.arange(ft_pad, dtype=jnp.int32)[None, :] == ft).astype(BF16)

    const2 = lambda a: pl.BlockSpec(a.shape, lambda b, i: (0,) * a.ndim)
    g2d = g.reshape(1, d)
    w_bf = w_out.astype(BF16)
    out = pl.pallas_call(
        _fourier_kernel,
        grid=(batch, nh),
        in_specs=[pl.BlockSpec((1, seq, d), lambda b, i: (b, 0, 0)),
                  const2(g2d), const2(ccsc),
                  pl.BlockSpec((1, ft_pad, seq), lambda b, i: (i, 0, 0)),
                  pl.BlockSpec((1, ft_pad, seq), lambda b, i: (i, 0, 0)),
                  const2(rev), const2(w_bf)],
        out_specs=pl.BlockSpec((1, 2, 1, ft, d), lambda b, i: (b, 0, i, 0, 0)),
        out_shape=jax.ShapeDtypeStruct((batch, 2, nh, ft, d), F32),
        scratch_shapes=[pltpu.VMEM((seq, d), BF16), pltpu.VMEM((seq, d), BF16)],
        compiler_params=_params(),
        name="fourier_mix",
    )(x2d.reshape(batch, seq, d), g2d, ccsc, ch, sh, rev, w_bf)
    return out.reshape(t, d)


def _mirror_tile_map(tm, seq):
    per = FOURIER_TILE // tm
    n_tiles = seq // FOURIER_TILE
    nh = n_tiles // 2
    tiles_per_seq = seq // tm

    def index_map(i):
        b, u = i // tiles_per_seq, i % tiles_per_seq
        j, sub = u // per, u % per
        stored = jnp.where(j < nh, j, nh + n_tiles - 1 - j)
        return (b * tiles_per_seq + stored * per + sub, 0)

    return index_map


def _ffn_kernel(x_ref, g_ref, wg_ref, wu_ref, wd_ref, gf_ref, o_ref, *, final_norm):
    x = x_ref[...]
    h = _rms(x, g_ref[...]).astype(BF16)
    gate = _dot(h, wg_ref[...])
    up = _dot(h, wu_ref[...])
    act = (gate * _sigmoid(gate) * up).astype(BF16)
    y = x + _dot(act, wd_ref[...])
    if final_norm:
        y = _rms(y, gf_ref[...])
    o_ref[...] = y


def _ffn(x2d, g, layer, w_gate, w_up, w_down, g_final, final_norm, x_index_map=None):
    t, d = x2d.shape
    f = w_gate.shape[2]
    tm = ROW_TILE
    if x_index_map is None:
        x_index_map = lambda i: (i, 0)
    else:
        x_index_map = x_index_map(tm)
    resident = lambda rows, cols: pl.BlockSpec((None, rows, cols), lambda i: (layer, 0, 0),
                                               pipeline_mode=pl.Buffered(1))
    return pl.pallas_call(
        functools.partial(_ffn_kernel, final_norm=final_norm),
        grid=(t // tm,),
        in_specs=[pl.BlockSpec((tm, d), x_index_map),
                  pl.BlockSpec((1, d), lambda i: (0, 0)),
                  resident(d, f),
                  resident(d, f),
                  resident(f, d),
                  pl.BlockSpec((1, d), lambda i: (0, 0))],
        out_specs=pl.BlockSpec((tm, d), lambda i: (i, 0)),
        out_shape=jax.ShapeDtypeStruct((t, d), F32),
        compiler_params=_params(),
        name="swiglu_ffn",
    )(x2d, g.reshape(1, d), w_gate, w_up, w_down, g_final.reshape(1, d))


def _rwkv_prep_kernel(x_ref, xp_ref, xn_ref, g_ref, mu_ref, sbar_ref, wrkv_ref, lora_ref, w2_ref, a2_ref,
                      g2_ref, w0_ref, a0_ref, kk_ref, ka_ref, rk_ref, ones_ref,
                      rvk_o, gate_o, bonus_o, kb0_o, kb1_o, lw0_o, lw1_o, *, tiles_per_seq):
    i = pl.program_id(0)
    tm, d = x_ref.shape
    g = g_ref[...]
    h = _rms(x_ref[...], g)
    hb = h.astype(BF16)
    pos = i % tiles_per_seq
    keep_prev = jnp.where(pos == 0, 0.0, 0.5)
    keep_next = jnp.where(pos == tiles_per_seq - 1, 0.0, 0.5)
    h_prev = _rms(xp_ref[...], g)[7:8, :] * keep_prev
    h_next = _rms(xn_ref[...], g)[0:1, :] * keep_next
    hbar = _dot(sbar_ref[...], hb)
    row8 = lax.broadcasted_iota(jnp.int32, (8, d), 0)
    hbar = jnp.concatenate([hbar[:8] + jnp.where(row8 == 0, h_prev, 0.0), hbar[8:tm - 8],
                            hbar[tm - 8:] + jnp.where(row8 == 7, h_next, 0.0)], axis=0)
    xx = (hbar - h).astype(BF16)
    mu = mu_ref[...].astype(BF16)

    def mix(c):
        return hb + xx * mu[c:c + 1, :]

    xr, xk, xv = mix(0), mix(1), mix(2)
    lo = _dot(jnp.concatenate([hb, xx], axis=1), lora_ref[...])
    nl = lo.shape[1] // 3
    w_lo = jnp.tanh(lo[:, :nl]).astype(BF16)
    a_lo = lo[:, nl:2 * nl].astype(BF16)
    g_lo = _sigmoid(lo[:, 2 * nl:]).astype(BF16)
    ones_bd = ones_ref[...]

    def cols_of(grp, part=0):
        return slice(part * d + grp * GROUP_W, part * d + (grp + 1) * GROUP_W)

    def dots(grp):
        cols = cols_of(grp)
        r = _dot(xr, wrkv_ref[0, :, cols])
        v = _dot(xv, wrkv_ref[2, :, cols])
        rvk_o[:, cols] = r.astype(BF16)
        rvk_o[:, cols_of(grp, 1)] = v.astype(BF16)
        gate_o[:, cols] = _dot(g_lo, g2_ref[:, cols]).astype(BF16)
        return (r, _dot(xk, wrkv_ref[1, :, cols]), v,
                [_dot(w_lo, w2_ref[j, :, cols]) for j in range(2)],
                [_dot(a_lo, a2_ref[j, :, cols]) for j in range(2)])

    def tail(grp, r, k, v, w_lora, a_lora):
        cols = cols_of(grp)
        kkr = k * kk_ref[:, cols]
        kkn = kkr * lax.rsqrt(jnp.maximum(_group_head_sums(kkr * kkr, ones_bd), 1e-24))
        rvk_o[:, cols_of(grp, 2)] = kkn.astype(BF16)
        k_b = k * ka_ref[:, cols]
        k_a = k - k_b
        k_sum = None
        for j, (kb_o, lw_o) in enumerate(((kb0_o, lw0_o), (kb1_o, lw1_o))):
            lw_o[:, cols] = -DECAY_SCALE * _sigmoid(w0_ref[j:j + 1, cols] + w_lora[j])
            a = _sigmoid(a0_ref[j:j + 1, cols] + a_lora[j])
            k_dir = k_a + k_b * a
            kb_o[:, cols] = k_dir.astype(BF16)
            kb_o[:, cols_of(grp, 1)] = (kkn * a).astype(BF16)
            k_sum = k_dir if k_sum is None else k_sum + k_dir
        bonus_o[:, cols] = (_group_head_sums(r * k_sum * rk_ref[:, cols], ones_bd) * v).astype(BF16)

    pending = dots(0)
    for grp in range(N_GROUPS):
        nxt = dots(grp + 1) if grp + 1 < N_GROUPS else None
        tail(grp, *pending)
        pending = nxt


def _wkv_consts(reverse):
    row = lax.broadcasted_iota(jnp.int32, (CHUNK, GROUP_W), 0)
    s = lax.broadcasted_iota(jnp.int32, (CHUNK, GROUP_W), 1) % HEAD
    ri = lax.broadcasted_iota(jnp.int32, (CHUNK, CHUNK), 0)
    ci = lax.broadcasted_iota(jnp.int32, (CHUNK, CHUNK), 1)
    if reverse:
        strict, incl, tri = s > row, s >= row, ci >= ri
    else:
        strict, incl, tri = s < row, s <= row, ci <= ri
    return dict(strict=strict, incl=incl, eye=s == row, tri=jnp.where(tri, 1.0, 0.0).astype(BF16))


def _wkv_kernel(rvkf, kbf, lwf, rvkb, kbb, lwb, yf_o, yb_o, state, *, cps, wave):
    @pl.when(pl.program_id(1) == 0)
    def _():
        state[...] = jnp.zeros_like(state)

    d_model = lwf.shape[1]
    consts = (_wkv_consts(False), _wkv_consts(True))
    dirs = ((rvkf, kbf, lwf, yf_o), (rvkb, kbb, lwb, yb_o))
    inst = [(d, g) for d in range(2) for g in range(N_GROUPS)]
    n = range(len(inst))
    half_w = GROUP_W // 2
    low_half = lax.broadcasted_iota(jnp.int32, (CHUNK, half_w), 1) < HEAD
    zero_half = jnp.zeros((CHUNK, half_w), BF16)

    def bd(y):
        yb = y.astype(BF16)
        parts = []
        for p in range(GROUP_HEADS):
            side = p // 2
            half = yb[:, side * half_w:(side + 1) * half_w]
            kept = jnp.where(low_half if p % 2 == 0 else ~low_half, half, zero_half)
            parts.append(jnp.concatenate([kept, zero_half] if side == 0 else [zero_half, kept], axis=1))
        return jnp.concatenate(parts, axis=0)

    def tr(y):
        yt = y.T
        return jnp.concatenate([yt[p * HEAD:(p + 1) * HEAD] for p in range(GROUP_HEADS)], axis=1)

    def mm(x, y):
        return _dot(x, bd(y))

    def cat(*xs):
        return jnp.concatenate(xs, axis=0)

    def bf(y):
        return y.astype(BF16)

    def rows_of(d, k):
        j = k if d == 0 else cps - 1 - k
        return slice(j * CHUNK, (j + 1) * CHUNK)

    def cols_of(g, part=0):
        return slice(part * d_model + g * GROUP_W, part * d_model + (g + 1) * GROUP_W)

    def prologue(k, out):
        e = [dict(), dict()]
        for d in range(2):
            lw = dirs[d][2][rows_of(d, k), :]
            hi = lw.astype(BF16)
            lo = (lw - hi.astype(F32)).astype(BF16)
            tri = consts[d]["tri"]
            cum = _dot(tri, hi) + _dot(tri, lo)
            last = cum[0:1, :] if d == 1 else cum[CHUNK - 1:CHUNK, :]
            e[d].update(lw=lw, cum=cum, last=last)
        yield
        for d in range(2):
            e[d].update(e_pos=jnp.exp(e[d]["cum"]), e_neg=jnp.exp(-e[d]["cum"]))
        yield
        for d in range(2):
            e[d].update(e_excl=jnp.exp(e[d]["cum"] - e[d]["lw"]), e_last=jnp.exp(e[d]["last"] - e[d]["cum"]),
                        w_last=jnp.exp(e[d]["last"]))
        yield

        def load(ref_idx, part, d, g):
            return dirs[d][ref_idx][rows_of(d, k), cols_of(g, part)].astype(F32)

        def ex(name, d, g):
            return e[d][name][:, cols_of(g)]

        out["rd"] = [load(0, 0, d, g) * ex("e_pos", d, g) for d, g in inst]
        out["kkd"] = [bf(load(0, 2, d, g) * ex("e_excl", d, g)) for d, g in inst]
        out["v"] = [dirs[d][0][rows_of(d, k), cols_of(g, 1)] for d, g in inst]
        out["w_last"] = [ex("w_last", d, g) for d, g in inst]
        yield
        kd = [load(1, 0, d, g) for d, g in inst]
        out["kdiv_t"] = [bf(tr(kd[i] * ex("e_neg", *inst[i]))) for i in n]
        out["kw_t"] = [bf(tr(kd[i] * ex("e_last", *inst[i]))) for i in n]
        yield
        b = [load(1, 1, d, g) for d, g in inst]
        out["bdiv_t"] = [bf(tr(b[i] * ex("e_neg", *inst[i]))) for i in n]
        out["bw_t"] = [bf(tr(b[i] * ex("e_last", *inst[i]))) for i in n]
        yield

    def stages(wave_chunks, st):
        items = [(k, i) for k in wave_chunks for i in n]
        m = range(len(items))
        c = [consts[inst[i][0]] for _, i in items]

        def pr(name):
            return [pro[k][name][i] for k, i in items]

        rd, kkd, v = pr("rd"), pr("kkd"), pr("v")
        kdiv_t, bdiv_t, kw_t, bw_t, w_last = pr("kdiv_t"), pr("bdiv_t"), pr("kw_t"), pr("bw_t"), pr("w_last")
        x = [cat(kkd[j], bf(rd[j])) for j in m]
        sk = [mm(x[j], kdiv_t[j]) for j in m]
        yield
        sb = [mm(x[j], bdiv_t[j]) for j in m]
        yield
        m_kk = [bf(jnp.where(c[j]["strict"], sk[j][:CHUNK], 0.0)) for j in m]
        a_rk = [bf(jnp.where(c[j]["incl"], sk[j][CHUNK:], 0.0)) for j in m]
        a_rb = [bf(jnp.where(c[j]["incl"], sb[j][CHUNK:], 0.0)) for j in m]
        a = [jnp.where(c[j]["strict"], -sb[j][:CHUNK], 0.0) for j in m]
        tinv = [jnp.where(c[j]["eye"], 1.0, 0.0) + a[j] for j in m]
        ap = [bf(a[j]) for j in m]
        ap = [bf(mm(ap[j], ap[j])) for j in m]
        yield
        for step in range(5):
            if step < 4:
                both = [mm(cat(bf(tinv[j]), ap[j]), ap[j]) for j in m]
                tinv = [tinv[j] + both[j][:CHUNK] for j in m]
                ap = [bf(both[j][CHUNK:]) for j in m]
            else:
                tinv = [tinv[j] + mm(bf(tinv[j]), ap[j]) for j in m]
            yield
        xv = [mm(cat(m_kk[j], a_rk[j], kw_t[j]), v[j]) for j in m]
        yield
        uz = [bf(mm(cat(a_rb[j], bw_t[j]), tinv[j])) for j in m]
        yield
        uzp = [mm(uz[j], kkd[j]) for j in m]
        uzq = [mm(uz[j], xv[j][:CHUNK]) for j in m]
        yield
        rp = [rd[j] - uzp[j][:CHUNK] for j in m]
        g_ = [jnp.where(c[j]["eye"], w_last[j], 0.0) - uzp[j][CHUNK:] for j in m]
        y0 = [xv[j][CHUNK:2 * CHUNK] - uzq[j][:CHUNK] for j in m]
        h_ = [xv[j][2 * CHUNK:] - uzq[j][CHUNK:] for j in m]
        for j, (k, i) in enumerate(items):
            d, g = inst[i]
            both = mm(bf(cat(g_[j], rp[j])), st[i])
            st[i] = both[:CHUNK] + h_[j]
            dirs[d][3][rows_of(d, k), cols_of(g)] = (both[CHUNK:] + y0[j]).astype(BF16)
        yield

    def run_all(gens):
        for gen in gens:
            for _ in gen:
                pass

    pro = [dict() for _ in range(cps)]
    waves = [list(range(w, min(w + wave, cps))) for w in range(0, cps, wave)]
    run_all([prologue(k, pro[k]) for k in waves[0]])
    st = [state[d, g] for d, g in inst]
    for wi, wv in enumerate(waves):
        side = [prologue(k, pro[k]) for k in waves[wi + 1]] if wi + 1 < len(waves) else []
        for _ in stages(wv, st):
            for gen in side:
                next(gen, None)
        run_all(side)
    for i, (d, g) in enumerate(inst):
        state[d, g] = st[i]


def _rwkv_post_kernel(yf_ref, yb_ref, bonus_ref, gate_ref, x_ref, lnw_ref, lnb_ref, wo_ref, ones_ref, o_ref):
    ones_bd = ones_ref[...]
    y = yf_ref[...].astype(F32) + yb_ref[...].astype(F32)
    mean = _head_sums(y, ones_bd) * (1.0 / HEAD)
    dlt = y - mean
    var = _head_sums(dlt * dlt, ones_bd) * (1.0 / HEAD)
    yn = dlt * lax.rsqrt(var + GN_EPS) * lnw_ref[...] + lnb_ref[...]
    out = ((yn + bonus_ref[...].astype(F32)) * gate_ref[...].astype(F32)).astype(BF16)
    o_ref[...] = x_ref[...] + _dot(out, wo_ref[...])


def _rwkv_mix(x2d, g, mu, w_rkv, w_o, w0, w1, w2, a0, a1, a2, g1, g2, k_k, k_a, r_k, ln_w, ln_b,
              batch, seq):
    t, d = x2d.shape
    tm = ROW_TILE
    tiles_per_seq = seq // tm
    lora = w1.shape[-1]
    n_dirs = w1.shape[0]
    row = lambda vec: vec.reshape(1, d)

    w1cat = jnp.concatenate([w1[j] for j in range(n_dirs)], axis=1)
    a1cat = jnp.concatenate([a1[j] for j in range(n_dirs)], axis=1)
    lora_in = jnp.concatenate([w1cat, a1cat, g1], axis=1)
    lora_mu = jnp.concatenate([mu[3][:, None] * w1cat, mu[4][:, None] * a1cat, mu[5][:, None] * g1], axis=1)
    lora_cat = jnp.concatenate([lora_in, lora_mu], axis=0).astype(BF16)
    w2pad = jnp.stack([jnp.pad(w2[j], ((j * lora, (n_dirs - 1 - j) * lora), (0, 0))) for j in range(n_dirs)]).astype(BF16)
    a2pad = jnp.stack([jnp.pad(a2[j], ((j * lora, (n_dirs - 1 - j) * lora), (0, 0))) for j in range(n_dirs)]).astype(BF16)
    li = jnp.arange(GROUP_W, dtype=jnp.int32) // HEAD
    ones_bd = (li[:, None] == li[None, :]).astype(BF16)

    full = lambda a: pl.BlockSpec(a.shape, lambda i: (0,) * a.ndim)
    tile = lambda part: pl.BlockSpec((tm, d), lambda i: (i, part))
    wide = lambda parts: pl.BlockSpec((tm, parts * d), lambda i: (i, 0))
    n8 = t // 8
    prev8 = pl.BlockSpec((8, d), lambda i: (jnp.maximum(i * (tm // 8) - 1, 0), 0))
    next8 = pl.BlockSpec((8, d), lambda i: (jnp.minimum((i + 1) * (tm // 8), n8 - 1), 0))
    ti = jnp.arange(tm, dtype=jnp.int32)
    sbar = jnp.where(jnp.abs(ti[:, None] - ti[None, :]) == 1, 0.5, 0.0).astype(BF16)
    consts = (row(g), mu, sbar, w_rkv.astype(BF16), lora_cat, w2pad, a2pad, g2.astype(BF16),
              w0, a0, row(k_k), row(k_a), row(r_k), ones_bd)
    bf_out = lambda parts: jax.ShapeDtypeStruct((t, parts * d), BF16)
    f32_out = jax.ShapeDtypeStruct((t, d), F32)
    rvk, gate, bonus, kb0, kb1, lw0, lw1 = pl.pallas_call(
        functools.partial(_rwkv_prep_kernel, tiles_per_seq=tiles_per_seq),
        grid=(t // tm,),
        in_specs=[tile(0), prev8, next8] + [full(a) for a in consts],
        out_specs=[wide(3), tile(0), tile(0), wide(2), wide(2), tile(0), tile(0)],
        out_shape=[bf_out(3), bf_out(1), bf_out(1), bf_out(2), bf_out(2), f32_out, f32_out],
        compiler_params=_params(),
        name="rwkv_prep",
    )(x2d, x2d, x2d, *consts)

    cps, wave = 8, 2
    nc = seq // (CHUNK * cps)
    fwd = lambda parts: pl.BlockSpec((CHUNK * cps, parts * d), lambda b, c: (b * nc + c, 0))
    bwd = lambda parts: pl.BlockSpec((CHUNK * cps, parts * d), lambda b, c: (b * nc + nc - 1 - c, 0))
    yf, yb = pl.pallas_call(
        functools.partial(_wkv_kernel, cps=cps, wave=wave),
        grid=(batch, nc),
        in_specs=[fwd(3), fwd(2), fwd(1), bwd(3), bwd(2), bwd(1)],
        out_specs=[fwd(1), bwd(1)],
        out_shape=[bf_out(1), bf_out(1)],
        scratch_shapes=[pltpu.VMEM((2, N_GROUPS, HEAD, GROUP_W), F32)],
        compiler_params=_params(),
        name="wkv7_chunked",
    )(rvk, kb0, lw0, rvk, kb1, lw1)

    consts = (row(ln_w), row(ln_b), w_o.astype(BF16), ones_bd)
    return pl.pallas_call(
        _rwkv_post_kernel,
        grid=(t // tm,),
        in_specs=[tile(0)] * 5 + [full(a) for a in consts],
        out_specs=tile(0),
        out_shape=f32_out,
        compiler_params=_params(),
        name="rwkv_post",
    )(yf, yb, bonus, gate, x2d, *consts)


def kernel(x, norm_mix_g, norm_ffn_g, norm_final_g, fno_w_out, rwkv_mu, rwkv_w_rkv, rwkv_w_o, rwkv_w0, rwkv_w1, rwkv_w2, rwkv_a0, rwkv_a1, rwkv_a2, rwkv_g1, rwkv_g2, rwkv_k_k, rwkv_k_a, rwkv_r_k, rwkv_ln_w, rwkv_ln_b, ffn_w_gate, ffn_w_up, ffn_w_down):
    batch, seq, d = x.shape
    depth = norm_mix_g.shape[0]
    h = x.reshape(batch * seq, d)
    w_gate, w_up, w_down = (w.astype(BF16) for w in (ffn_w_gate, ffn_w_up, ffn_w_down))
    for i in range(depth):
        j = i // 2
        x_index_map = None
        if i % 2 == 0:
            h = _fourier_mix(h, norm_mix_g[i], fno_w_out[j], batch, seq)
            x_index_map = functools.partial(_mirror_tile_map, seq=seq)
        else:
            h = _rwkv_mix(h, norm_mix_g[i], rwkv_mu[j], rwkv_w_rkv[j], rwkv_w_o[j], rwkv_w0[j], rwkv_w1[j],
                          rwkv_w2[j], rwkv_a0[j], rwkv_a1[j], rwkv_a2[j], rwkv_g1[j], rwkv_g2[j],
                          rwkv_k_k[j], rwkv_k_a[j], rwkv_r_k[j], rwkv_ln_w[j], rwkv_ln_b[j], batch, seq)
        h = _ffn(h, norm_ffn_g[i], i, w_gate, w_up, w_down, norm_final_g,
                 final_norm=(i == depth - 1), x_index_map=x_index_map)
    return h.reshape(batch, seq, d)
```

```python
import functools
import math

import jax
import jax.numpy as jnp
from jax import lax
from jax.experimental import pallas as pl
from jax.experimental.pallas import tpu as pltpu

F32 = jnp.float32
BF16 = jnp.bfloat16

D_MODEL = 1024
HEAD = 64
GROUP_HEADS = 4
GROUP_W = GROUP_HEADS * HEAD
N_GROUPS = D_MODEL // GROUP_W
CHUNK = 64
ROW_TILE = 512
FOURIER_GROUPS = 4
FOURIER_GW = D_MODEL // FOURIER_GROUPS
FOURIER_TILE = 512
BF16_ROWS = 16
RMS_EPS = 1e-6
GN_EPS = HEAD * 1e-5
DECAY_SCALE = math.exp(-0.5)
VMEM_LIMIT = 58 * 1024 * 1024


def _params():
    return pltpu.CompilerParams(vmem_limit_bytes=VMEM_LIMIT)


def _dot(a, b):
    return jnp.dot(a, b, preferred_element_type=F32)


def _rms(x, g):
    return x * lax.rsqrt(jnp.mean(x * x, axis=-1, keepdims=True) + RMS_EPS) * g


def _sigmoid(z):
    return 1.0 / (1.0 + jnp.exp(-z))


def _group_head_sums(ug, ones_bd):
    return _dot(ug.astype(BF16), ones_bd)


def _head_sums(u, ones_bd):
    return jnp.concatenate([_group_head_sums(u[:, g * GROUP_W:(g + 1) * GROUP_W], ones_bd)
                            for g in range(N_GROUPS)], axis=1)


def _fourier_kernel(x_ref, g_ref, ccsc_ref, ch_ref, sh_ref, rev_ref, w_ref, o_ref, zc, zs):
    i = pl.program_id(1)
    ft = FOURIER_TILE
    seq = x_ref.shape[1]
    n_tiles = seq // ft

    @pl.when(i == 0)
    def _():
        for t in range(n_tiles):
            rows = slice(t * ft, (t + 1) * ft)
            h = _rms(x_ref[0, rows, :], g_ref[...]).astype(BF16)
            for g in range(FOURIER_GROUPS):
                cols = slice(g * FOURIER_GW, (g + 1) * FOURIER_GW)
                z = _dot(h[:, cols], ccsc_ref[...])
                zc[rows, cols] = z[:, :FOURIER_GW].astype(BF16)
                zs[rows, cols] = z[:, FOURIER_GW:].astype(BF16)

    t1 = _dot(ch_ref[0], zc[...])
    t2 = _dot(sh_ref[0], zs[...])
    w = w_ref[...]
    top = (t1[:ft] - t2[:ft]).astype(BF16)
    r_top = pl.multiple_of(i * ft, ft)
    o_ref[0, 0, 0] = x_ref[0, pl.ds(r_top, ft), :] + _dot(top, w)
    bot = _dot(rev_ref[...], (t1 + t2).astype(BF16)).astype(BF16)
    r_bot = pl.multiple_of((n_tiles - 1 - i) * ft, ft)
    o_ref[0, 1, 0] = x_ref[0, pl.ds(r_bot, ft), :] + _dot(bot, w)


def _fourier_mix(x2d, g, w_out, batch, seq):
    t, d = x2d.shape
    ft = FOURIER_TILE
    ft_pad = ft + BF16_ROWS
    nh = seq // (2 * ft)
    n = jnp.arange(FOURIER_GW, dtype=jnp.int32)
    ang = ((n[:, None] * n[None, :]) % FOURIER_GW).astype(F32) * (2.0 * math.pi / FOURIER_GW)
    ccsc = (jnp.concatenate([jnp.cos(ang), jnp.sin(ang)], axis=1) * (FOURIER_GW ** -0.5)).astype(BF16)
    n = jnp.arange(seq, dtype=jnp.int32)
    blk = 32
    n_hi = -(-((nh - 1) * ft + ft_pad) // blk)

    def table(kvec):
        ang = ((kvec[:, None] * n[None, :]) % seq).astype(F32) * (2.0 * math.pi / seq)
        return jnp.cos(ang), jnp.sin(ang)

    c_hi, s_hi = (tab[:, None, :] for tab in table(jnp.arange(n_hi, dtype=jnp.int32) * blk))
    c_lo, s_lo = (tab[None, :, :] for tab in table(jnp.arange(blk, dtype=jnp.int32)))
    cos_k = ((c_hi * c_lo - s_hi * s_lo) * (seq ** -0.5)).reshape(n_hi * blk, seq)
    sin_k = ((s_hi * c_lo + c_hi * s_lo) * (seq ** -0.5)).reshape(n_hi * blk, seq)
    ch = jnp.stack([cos_k[i * ft:i * ft + ft_pad] for i in range(nh)]).astype(BF16)
    sh = jnp.stack([sin_k[i * ft:i * ft + ft_pad] for i in range(nh)]).astype(BF16)
    rev = (jnp.arange(ft, dtype=jnp.int32)[:, None] + jnp.arange(ft_pad, dtype=jnp.int32)[None, :] == ft).astype(BF16)

    const2 = lambda a: pl.BlockSpec(a.shape, lambda b, i: (0,) * a.ndim)
    g2d = g.reshape(1, d)
    w_bf = w_out.astype(BF16)
    out = pl.pallas_call(
        _fourier_kernel,
        grid=(batch, nh),
        in_specs=[pl.BlockSpec((1, seq, d), lambda b, i: (b, 0, 0)),
                  const2(g2d), const2(ccsc),
                  pl.BlockSpec((1, ft_pad, seq), lambda b, i: (i, 0, 0)),
                  pl.BlockSpec((1, ft_pad, seq), lambda b, i: (i, 0, 0)),
                  const2(rev), const2(w_bf)],
        out_specs=pl.BlockSpec((1, 2, 1, ft, d), lambda b, i: (b, 0, i, 0, 0)),
        out_shape=jax.ShapeDtypeStruct((batch, 2, nh, ft, d), F32),
        scratch_shapes=[pltpu.VMEM((seq, d), BF16), pltpu.VMEM((seq, d), BF16)],
        compiler_params=_params(),
        name="fourier_mix",
    )(x2d.reshape(batch, seq, d), g2d, ccsc, ch, sh, rev, w_bf)
    return out.reshape(t, d)


def _mirror_tile_map(tm, seq):
    per = FOURIER_TILE // tm
    n_tiles = seq // FOURIER_TILE
    nh = n_tiles // 2
    tiles_per_seq = seq // tm

    def index_map(i):
        b, u = i // tiles_per_seq, i % tiles_per_seq
        j, sub = u // per, u % per
        stored = jnp.where(j < nh, j, nh + n_tiles - 1 - j)
        return (b * tiles_per_seq + stored * per + sub, 0)

    return index_map


def _ffn_kernel(x_ref, g_ref, wg_ref, wu_ref, wd_ref, gf_ref, o_ref, *, final_norm):
    half = x_ref.shape[0] // 2
    rows = (slice(0, half), slice(half, 2 * half))

    def head(r):
        return _rms(x_ref[r, :], g_ref[...]).astype(BF16)

    def gate_up(h):
        return _dot(h, wg_ref[...]), _dot(h, wu_ref[...])

    def act(gate, up):
        return (gate * _sigmoid(gate) * up).astype(BF16)

    def tail(r, down):
        y = x_ref[r, :] + down
        if final_norm:
            y = _rms(y, gf_ref[...])
        o_ref[r, :] = y

    gu_a = gate_up(head(rows[0]))
    h_b = head(rows[1])
    act_a = act(*gu_a)
    gu_b = gate_up(h_b)
    down_a = _dot(act_a, wd_ref[...])
    act_b = act(*gu_b)
    tail(rows[0], down_a)
    tail(rows[1], _dot(act_b, wd_ref[...]))


def _ffn(x2d, g, layer, w_gate, w_up, w_down, g_final, final_norm, x_index_map=None):
    t, d = x2d.shape
    f = w_gate.shape[2]
    tm = ROW_TILE
    if x_index_map is None:
        x_index_map = lambda i: (i, 0)
    else:
        x_index_map = x_index_map(tm)
    resident = lambda rows, cols: pl.BlockSpec((None, rows, cols), lambda i: (layer, 0, 0),
                                               pipeline_mode=pl.Buffered(1))
    return pl.pallas_call(
        functools.partial(_ffn_kernel, final_norm=final_norm),
        grid=(t // tm,),
        in_specs=[pl.BlockSpec((tm, d), x_index_map),
                  pl.BlockSpec((1, d), lambda i: (0, 0)),
                  resident(d, f),
                  resident(d, f),
                  resident(f, d),
                  pl.BlockSpec((1, d), lambda i: (0, 0))],
        out_specs=pl.BlockSpec((tm, d), lambda i: (i, 0)),
        out_shape=jax.ShapeDtypeStruct((t, d), F32),
        compiler_params=_params(),
        name="swiglu_ffn",
    )(x2d, g.reshape(1, d), w_gate, w_up, w_down, g_final.reshape(1, d))


def _rwkv_prep_kernel(x_ref, xp_ref, xn_ref, g_ref, mu_ref, sbar_ref, wrkv_ref, lora_ref, w2_ref, a2_ref,
                      g2_ref, w0_ref, a0_ref, kk_ref, ka_ref, rk_ref, ones_ref,
                      rvk_o, gate_o, bonus_o, kb0_o, kb1_o, lw0_o, lw1_o, *, tiles_per_seq):
    i = pl.program_id(0)
    tm, d = x_ref.shape
    g = g_ref[...]
    h = _rms(x_ref[...], g)
    hb = h.astype(BF16)
    pos = i % tiles_per_seq
    keep_prev = jnp.where(pos == 0, 0.0, 0.5)
    keep_next = jnp.where(pos == tiles_per_seq - 1, 0.0, 0.5)
    h_prev = _rms(xp_ref[...], g)[7:8, :] * keep_prev
    h_next = _rms(xn_ref[...], g)[0:1, :] * keep_next
    hbar = _dot(sbar_ref[...], hb)
    row8 = lax.broadcasted_iota(jnp.int32, (8, d), 0)
    hbar = jnp.concatenate([hbar[:8] + jnp.where(row8 == 0, h_prev, 0.0), hbar[8:tm - 8],
                            hbar[tm - 8:] + jnp.where(row8 == 7, h_next, 0.0)], axis=0)
    xx = (hbar - h).astype(BF16)
    mu = mu_ref[...].astype(BF16)

    def mix(c):
        return hb + xx * mu[c:c + 1, :]

    xr, xk, xv = mix(0), mix(1), mix(2)
    lo = _dot(jnp.concatenate([hb, xx], axis=1), lora_ref[...])
    nl = lo.shape[1] // 3
    w_lo = jnp.tanh(lo[:, :nl]).astype(BF16)
    a_lo = lo[:, nl:2 * nl].astype(BF16)
    g_lo = _sigmoid(lo[:, 2 * nl:]).astype(BF16)
    ones_bd = ones_ref[...]

    def cols_of(grp, part=0):
        return slice(part * d + grp * GROUP_W, part * d + (grp + 1) * GROUP_W)

    def dots(grp):
        cols = cols_of(grp)
        r = _dot(xr, wrkv_ref[0, :, cols])
        v = _dot(xv, wrkv_ref[2, :, cols])
        rvk_o[:, cols] = r.astype(BF16)
        rvk_o[:, cols_of(grp, 1)] = v.astype(BF16)
        gate_o[:, cols] = _dot(g_lo, g2_ref[:, cols]).astype(BF16)
        return (r, _dot(xk, wrkv_ref[1, :, cols]), v,
                [_dot(w_lo, w2_ref[j, :, cols]) for j in range(2)],
                [_dot(a_lo, a2_ref[j, :, cols]) for j in range(2)])

    def tail(grp, r, k, v, w_lora, a_lora):
        cols = cols_of(grp)
        kkr = k * kk_ref[:, cols]
        kkn = kkr * lax.rsqrt(jnp.maximum(_group_head_sums(kkr * kkr, ones_bd), 1e-24))
        rvk_o[:, cols_of(grp, 2)] = kkn.astype(BF16)
        k_b = k * ka_ref[:, cols]
        k_a = k - k_b
        k_sum = None
        for j, (kb_o, lw_o) in enumerate(((kb0_o, lw0_o), (kb1_o, lw1_o))):
            lw_o[:, cols] = (-DECAY_SCALE * _sigmoid(w0_ref[j:j + 1, cols] + w_lora[j])).astype(BF16)
            a = _sigmoid(a0_ref[j:j + 1, cols] + a_lora[j])
            k_dir = k_a + k_b * a
            kb_o[:, cols] = k_dir.astype(BF16)
            kb_o[:, cols_of(grp, 1)] = (kkn * a).astype(BF16)
            k_sum = k_dir if k_sum is None else k_sum + k_dir
        bonus_o[:, cols] = (_group_head_sums(r * k_sum * rk_ref[:, cols], ones_bd) * v).astype(BF16)

    pending = dots(0)
    for grp in range(N_GROUPS):
        nxt = dots(grp + 1) if grp + 1 < N_GROUPS else None
        tail(grp, *pending)
        pending = nxt


def _wkv_consts(reverse):
    row = lax.broadcasted_iota(jnp.int32, (CHUNK, GROUP_W), 0)
    s = lax.broadcasted_iota(jnp.int32, (CHUNK, GROUP_W), 1) % HEAD
    ri = lax.broadcasted_iota(jnp.int32, (CHUNK, CHUNK), 0)
    ci = lax.broadcasted_iota(jnp.int32, (CHUNK, CHUNK), 1)
    if reverse:
        strict, incl, tri = s > row, s >= row, ci >= ri
    else:
        strict, incl, tri = s < row, s <= row, ci <= ri
    return dict(strict=strict, incl=incl, eye=s == row, tri=jnp.where(tri, 1.0, 0.0).astype(BF16))


def _wkv_kernel(rvkf, kbf, lwf, rvkb, kbb, lwb, yf_o, yb_o, state, *, cps, wave):
    @pl.when(pl.program_id(1) == 0)
    def _():
        state[...] = jnp.zeros_like(state)

    d_model = lwf.shape[1]
    consts = (_wkv_consts(False), _wkv_consts(True))
    dirs = ((rvkf, kbf, lwf, yf_o), (rvkb, kbb, lwb, yb_o))
    inst = [(d, g) for d in range(2) for g in range(N_GROUPS)]
    n = range(len(inst))
    half_w = GROUP_W // 2
    low_half = lax.broadcasted_iota(jnp.int32, (CHUNK, half_w), 1) < HEAD
    zero_half = jnp.zeros((CHUNK, half_w), BF16)

    def bd(y):
        yb = y.astype(BF16)
        parts = []
        for p in range(GROUP_HEADS):
            side = p // 2
            half = yb[:, side * half_w:(side + 1) * half_w]
            kept = jnp.where(low_half if p % 2 == 0 else ~low_half, half, zero_half)
            parts.append(jnp.concatenate([kept, zero_half] if side == 0 else [zero_half, kept], axis=1))
        return jnp.concatenate(parts, axis=0)

    def tr(y):
        yt = y.T
        return jnp.concatenate([yt[p * HEAD:(p + 1) * HEAD] for p in range(GROUP_HEADS)], axis=1)

    def mm(x, y):
        return _dot(x, bd(y))

    def cat(*xs):
        return jnp.concatenate(xs, axis=0)

    def bf(y):
        return y.astype(BF16)

    def rows_of(d, k):
        j = k if d == 0 else cps - 1 - k
        return slice(j * CHUNK, (j + 1) * CHUNK)

    def cols_of(g, part=0):
        return slice(part * d_model + g * GROUP_W, part * d_model + (g + 1) * GROUP_W)

    def prologue(k, out):
        e = [dict(), dict()]
        for d in range(2):
            lw = dirs[d][2][rows_of(d, k), :]
            cum = _dot(consts[d]["tri"], lw)
            last = cum[0:1, :] if d == 1 else cum[CHUNK - 1:CHUNK, :]
            e[d].update(lw=lw.astype(F32), cum=cum, last=last)
        yield
        for d in range(2):
            e[d].update(e_pos=jnp.exp(e[d]["cum"]), e_neg=jnp.exp(-e[d]["cum"]))
        yield
        for d in range(2):
            e[d].update(e_excl=jnp.exp(e[d]["cum"] - e[d]["lw"]), e_last=jnp.exp(e[d]["last"] - e[d]["cum"]),
                        w_last=jnp.exp(e[d]["last"]))
        yield

        def load(ref_idx, part, d, g):
            return dirs[d][ref_idx][rows_of(d, k), cols_of(g, part)].astype(F32)

        def ex(name, d, g):
            return e[d][name][:, cols_of(g)]

        out["rd"] = [load(0, 0, d, g) * ex("e_pos", d, g) for d, g in inst]
        out["kkd"] = [bf(load(0, 2, d, g) * ex("e_excl", d, g)) for d, g in inst]
        out["v"] = [dirs[d][0][rows_of(d, k), cols_of(g, 1)] for d, g in inst]
        out["w_last"] = [ex("w_last", d, g) for d, g in inst]
        yield
        kd = [load(1, 0, d, g) for d, g in inst]
        out["kdiv_t"] = [bf(tr(kd[i] * ex("e_neg", *inst[i]))) for i in n]
        out["kw_t"] = [bf(tr(kd[i] * ex("e_last", *inst[i]))) for i in n]
        yield
        b = [load(1, 1, d, g) for d, g in inst]
        out["bdiv_t"] = [bf(tr(b[i] * ex("e_neg", *inst[i]))) for i in n]
        out["bw_t"] = [bf(tr(b[i] * ex("e_last", *inst[i]))) for i in n]
        yield

    def stages(wave_chunks, st):
        items = [(k, i) for k in wave_chunks for i in n]
        m = range(len(items))
        c = [consts[inst[i][0]] for _, i in items]

        def pr(name):
            return [pro[k][name][i] for k, i in items]

        rd, kkd, v = pr("rd"), pr("kkd"), pr("v")
        kdiv_t, bdiv_t, kw_t, bw_t, w_last = pr("kdiv_t"), pr("bdiv_t"), pr("kw_t"), pr("bw_t"), pr("w_last")
        x = [cat(kkd[j], bf(rd[j])) for j in m]
        sk = [mm(x[j], kdiv_t[j]) for j in m]
        yield
        sb = [mm(x[j], bdiv_t[j]) for j in m]
        yield
        m_kk = [bf(jnp.where(c[j]["strict"], sk[j][:CHUNK], 0.0)) for j in m]
        a_rk = [bf(jnp.where(c[j]["incl"], sk[j][CHUNK:], 0.0)) for j in m]
        a_rb = [bf(jnp.where(c[j]["incl"], sb[j][CHUNK:], 0.0)) for j in m]
        a = [jnp.where(c[j]["strict"], -sb[j][:CHUNK], 0.0) for j in m]
        tinv = [jnp.where(c[j]["eye"], 1.0, 0.0) + a[j] for j in m]
        ap = [bf(a[j]) for j in m]
        ap = [bf(mm(ap[j], ap[j])) for j in m]
        yield
        for step in range(5):
            if step < 4:
                both = [mm(cat(bf(tinv[j]), ap[j]), ap[j]) for j in m]
                tinv = [tinv[j] + both[j][:CHUNK] for j in m]
                ap = [bf(both[j][CHUNK:]) for j in m]
            else:
                tinv = [tinv[j] + mm(bf(tinv[j]), ap[j]) for j in m]
            yield
        xv = [mm(cat(m_kk[j], a_rk[j], kw_t[j]), v[j]) for j in m]
        yield
        uz = [bf(mm(cat(a_rb[j], bw_t[j]), tinv[j])) for j in m]
        yield
        uzp = [mm(uz[j], kkd[j]) for j in m]
        uzq = [mm(uz[j], xv[j][:CHUNK]) for j in m]
        yield
        rp = [rd[j] - uzp[j][:CHUNK] for j in m]
        g_ = [jnp.where(c[j]["eye"], w_last[j], 0.0) - uzp[j][CHUNK:] for j in m]
        y0 = [xv[j][CHUNK:2 * CHUNK] - uzq[j][:CHUNK] for j in m]
        h_ = [xv[j][2 * CHUNK:] - uzq[j][CHUNK:] for j in m]
        for j, (k, i) in enumerate(items):
            d, g = inst[i]
            both = mm(bf(cat(g_[j], rp[j])), st[i])
            st[i] = both[:CHUNK] + h_[j]
            dirs[d][3][rows_of(d, k), cols_of(g)] = (both[CHUNK:] + y0[j]).astype(BF16)
        yield

    def run_all(gens):
        for gen in gens:
            for _ in gen:
                pass

    pro = [dict() for _ in range(cps)]
    waves = [list(range(w, min(w + wave, cps))) for w in range(0, cps, wave)]
    run_all([prologue(k, pro[k]) for k in waves[0]])
    st = [state[d, g] for d, g in inst]
    for wi, wv in enumerate(waves):
        side = [prologue(k, pro[k]) for k in waves[wi + 1]] if wi + 1 < len(waves) else []
        for _ in stages(wv, st):
            for gen in side:
                next(gen, None)
        run_all(side)
    for i, (d, g) in enumerate(inst):
        state[d, g] = st[i]


def _rwkv_post_kernel(yf_ref, yb_ref, bonus_ref, gate_ref, x_ref, lnw_ref, lnb_ref, wo_ref, ones_ref, o_ref):
    ones_bd = ones_ref[...]
    y = yf_ref[...].astype(F32) + yb_ref[...].astype(F32)
    mean = _head_sums(y, ones_bd) * (1.0 / HEAD)
    dlt = y - mean
    var = _head_sums(dlt * dlt, ones_bd) * (1.0 / HEAD)
    yn = dlt * lax.rsqrt(var + GN_EPS) * lnw_ref[...] + lnb_ref[...]
    out = ((yn + bonus_ref[...].astype(F32)) * gate_ref[...].astype(F32)).astype(BF16)
    o_ref[...] = x_ref[...] + _dot(out, wo_ref[...])


def _rwkv_mix(x2d, g, mu, w_rkv, w_o, w0, w1, w2, a0, a1, a2, g1, g2, k_k, k_a, r_k, ln_w, ln_b,
              batch, seq):
    t, d = x2d.shape
    tm = ROW_TILE
    tiles_per_seq = seq // tm
    lora = w1.shape[-1]
    n_dirs = w1.shape[0]
    row = lambda vec: vec.reshape(1, d)

    w1cat = jnp.concatenate([w1[j] for j in range(n_dirs)], axis=1)
    a1cat = jnp.concatenate([a1[j] for j in range(n_dirs)], axis=1)
    lora_in = jnp.concatenate([w1cat, a1cat, g1], axis=1)
    lora_mu = jnp.concatenate([mu[3][:, None] * w1cat, mu[4][:, None] * a1cat, mu[5][:, None] * g1], axis=1)
    lora_cat = jnp.concatenate([lora_in, lora_mu], axis=0).astype(BF16)
    w2pad = jnp.stack([jnp.pad(w2[j], ((j * lora, (n_dirs - 1 - j) * lora), (0, 0))) for j in range(n_dirs)]).astype(BF16)
    a2pad = jnp.stack([jnp.pad(a2[j], ((j * lora, (n_dirs - 1 - j) * lora), (0, 0))) for j in range(n_dirs)]).astype(BF16)
    li = jnp.arange(GROUP_W, dtype=jnp.int32) // HEAD
    ones_bd = (li[:, None] == li[None, :]).astype(BF16)

    full = lambda a: pl.BlockSpec(a.shape, lambda i: (0,) * a.ndim)
    tile = lambda part: pl.BlockSpec((tm, d), lambda i: (i, part))
    wide = lambda parts: pl.BlockSpec((tm, parts * d), lambda i: (i, 0))
    n8 = t // 8
    prev8 = pl.BlockSpec((8, d), lambda i: (jnp.maximum(i * (tm // 8) - 1, 0), 0))
    next8 = pl.BlockSpec((8, d), lambda i: (jnp.minimum((i + 1) * (tm // 8), n8 - 1), 0))
    ti = jnp.arange(tm, dtype=jnp.int32)
    sbar = jnp.where(jnp.abs(ti[:, None] - ti[None, :]) == 1, 0.5, 0.0).astype(BF16)
    consts = (row(g), mu, sbar, w_rkv.astype(BF16), lora_cat, w2pad, a2pad, g2.astype(BF16),
              w0, a0, row(k_k), row(k_a), row(r_k), ones_bd)
    bf_out = lambda parts: jax.ShapeDtypeStruct((t, parts * d), BF16)
    f32_out = jax.ShapeDtypeStruct((t, d), F32)
    rvk, gate, bonus, kb0, kb1, lw0, lw1 = pl.pallas_call(
        functools.partial(_rwkv_prep_kernel, tiles_per_seq=tiles_per_seq),
        grid=(t // tm,),
        in_specs=[tile(0), prev8, next8] + [full(a) for a in consts],
        out_specs=[wide(3), tile(0), tile(0), wide(2), wide(2), tile(0), tile(0)],
        out_shape=[bf_out(3), bf_out(1), bf_out(1), bf_out(2), bf_out(2), bf_out(1), bf_out(1)],
        compiler_params=_params(),
        name="rwkv_prep",
    )(x2d, x2d, x2d, *consts)

    cps, wave = 8, 2
    nc = seq // (CHUNK * cps)
    fwd = lambda parts: pl.BlockSpec((CHUNK * cps, parts * d), lambda b, c: (b * nc + c, 0))
    bwd = lambda parts: pl.BlockSpec((CHUNK * cps, parts * d), lambda b, c: (b * nc + nc - 1 - c, 0))
    yf, yb = pl.pallas_call(
        functools.partial(_wkv_kernel, cps=cps, wave=wave),
        grid=(batch, nc),
        in_specs=[fwd(3), fwd(2), fwd(1), bwd(3), bwd(2), bwd(1)],
        out_specs=[fwd(1), bwd(1)],
        out_shape=[bf_out(1), bf_out(1)],
        scratch_shapes=[pltpu.VMEM((2, N_GROUPS, HEAD, GROUP_W), F32)],
        compiler_params=_params(),
        name="wkv7_chunked",
    )(rvk, kb0, lw0, rvk, kb1, lw1)

    consts = (row(ln_w), row(ln_b), w_o.astype(BF16), ones_bd)
    return pl.pallas_call(
        _rwkv_post_kernel,
        grid=(t // tm,),
        in_specs=[tile(0)] * 5 + [full(a) for a in consts],
        out_specs=tile(0),
        out_shape=f32_out,
        compiler_params=_params(),
        name="rwkv_post",
    )(yf, yb, bonus, gate, x2d, *consts)


def kernel(x, norm_mix_g, norm_ffn_g, norm_final_g, fno_w_out, rwkv_mu, rwkv_w_rkv, rwkv_w_o, rwkv_w0, rwkv_w1, rwkv_w2, rwkv_a0, rwkv_a1, rwkv_a2, rwkv_g1, rwkv_g2, rwkv_k_k, rwkv_k_a, rwkv_r_k, rwkv_ln_w, rwkv_ln_b, ffn_w_gate, ffn_w_up, ffn_w_down):
    batch, seq, d = x.shape
    depth = norm_mix_g.shape[0]
    h = x.reshape(batch * seq, d)
    w_gate, w_up, w_down = (w.astype(BF16) for w in (ffn_w_gate, ffn_w_up, ffn_w_down))
    for i in range(depth):
        j = i // 2
        x_index_map = None
        if i % 2 == 0:
            h = _fourier_mix(h, norm_mix_g[i], fno_w_out[j], batch, seq)
            x_index_map = functools.partial(_mirror_tile_map, seq=seq)
        else:
            h = _rwkv_mix(h, norm_mix_g[i], rwkv_mu[j], rwkv_w_rkv[j], rwkv_w_o[j], rwkv_w0[j], rwkv_w1[j],
                          rwkv_w2[j], rwkv_a0[j], rwkv_a1[j], rwkv_a2[j], rwkv_g1[j], rwkv_g2[j],
                          rwkv_k_k[j], rwkv_k_a[j], rwkv_r_k[j], rwkv_ln_w[j], rwkv_ln_b[j], batch, seq)
        h = _ffn(h, norm_ffn_g[i], i, w_gate, w_up, w_down, norm_final_g,
                 final_norm=(i == depth - 1), x_index_map=x_index_map)
    return h.reshape(batch, seq, d)
```
